```python
import math
import jax, jax.numpy as jnp
from jax import lax
import numpy as np

D_MODEL = 1024
BATCH = 2
SEQ = 8192
DEPTH = 2

GLA_HEADS = 4
GLA_DK = 64
GLA_DV = 128
GLA_LR = 16
GLA_GATE_TAU = 16.0
GLA_CHUNK = 64
S5_WIDTH = 512
S5_GROUP = 16
S5_GROUPS = S5_WIDTH // S5_GROUP
S5_STATE = 64
S5_DT_MIN = 0.001
S5_DT_MAX = 0.1
DIFF_HEADS = 4
DIFF_DH = 64
DIFF_DV = 2 * DIFF_DH
QBLOCK = 128
D_FF = 2816
CONV_W = 3
EPS = 1e-6

GLA_QK_COLS = GLA_HEADS * GLA_DK
GLA_V_COLS = GLA_HEADS * GLA_DV
DIFF_QK_COLS = DIFF_HEADS * 2 * DIFF_DH
DIFF_V_COLS = DIFF_HEADS * DIFF_DV
IN_SPLITS = (GLA_QK_COLS, GLA_QK_COLS, GLA_V_COLS, GLA_V_COLS, GLA_LR, S5_WIDTH,
             DIFF_QK_COLS, DIFF_QK_COLS, DIFF_V_COLS, 3 * D_MODEL)
IN_COLS = sum(IN_SPLITS)

kernel_name = 'hybrid_gla_s5_diffattn_block'


def _split_points():
    pts, acc = [], 0
    for s in IN_SPLITS[:-1]:
        acc += s
        pts.append(acc)
    return pts


def rms_norm(x, g):
    xf = x.astype(jnp.float32)
    y = xf * lax.rsqrt(jnp.mean(xf * xf, axis=-1, keepdims=True) + EPS)
    return (y * g.astype(jnp.float32)).astype(x.dtype)


def gla_mixer(q, k, v, r, g_lr, gk_w, gk_b, norm_g):
    f32 = jnp.float32
    Bn, S, _ = q.shape
    N = S // GLA_CHUNK

    def heads(t, d):
        return t.astype(f32).reshape(Bn, N, GLA_CHUNK, GLA_HEADS, d).transpose(0, 3, 1, 2, 4)

    log_a = jax.nn.log_sigmoid((g_lr @ gk_w + gk_b).astype(f32)) / GLA_GATE_TAU
    qh = heads(q, GLA_DK) * GLA_DK ** -0.5
    kh = heads(k, GLA_DK)
    vh = heads(v, GLA_DV)
    b = jnp.cumsum(heads(log_a, GLA_DK), axis=3)
    b_last = b[:, :, :, -1:, :]
    q_dec = qh * jnp.exp(b)
    causal = jnp.tril(jnp.ones((GLA_CHUNK, GLA_CHUNK), bool))
    attn = jnp.einsum('bhncd,bhnsd->bhncs', q_dec, kh * jnp.exp(-b))
    attn = jnp.where(causal, attn, 0.0)
    o_intra = jnp.einsum('bhncs,bhnse->bhnce', attn, vh)
    d_state = jnp.einsum('bhncd,bhnce->nbhde', kh * jnp.exp(b_last - b), vh)
    decay = jnp.exp(b_last[:, :, :, 0, :]).transpose(2, 0, 1, 3)

    def step(state, inp):
        dec, ds = inp
        return dec[..., None] * state + ds, state

    s0 = jnp.zeros((Bn, GLA_HEADS, GLA_DK, GLA_DV), f32)
    _, s_in = lax.scan(step, s0, (decay, d_state))
    o_inter = jnp.einsum('bhncd,nbhde->bhnce', q_dec, s_in)
    o = (o_intra + o_inter).transpose(0, 2, 3, 1, 4)
    o = rms_norm(o, norm_g).reshape(Bn, S, GLA_V_COLS)
    return (o * jax.nn.silu(r.astype(f32))).astype(q.dtype)


def s5_mixer(u, lam_re, lam_im, b_re, b_im, c_re, c_im, d_skip, log_dt, glu_w, glu_b):
    f32 = jnp.float32
    Bn, S, _ = u.shape
    uf = u.astype(f32)
    dt = jnp.exp(log_dt.astype(f32))[:, None]
    lam = lax.complex(lam_re.astype(f32), lam_im.astype(f32))
    lam_bar = jnp.exp(lam * dt)
    b_bar = ((lam_bar - 1.0) / lam)[..., None] * lax.complex(b_re.astype(f32), b_im.astype(f32))
    ug = uf.reshape(Bn, S, S5_GROUPS, S5_GROUP).astype(jnp.complex64)
    bu = jnp.einsum('bsgh,gph->bsgp', ug, b_bar)
    a = jnp.broadcast_to(lam_bar, bu.shape)

    def combine(e1, e2):
        a1, b1 = e1
        a2, b2 = e2
        return a2 * a1, a2 * b1 + b2

    _, xs = lax.associative_scan(combine, (a, bu), axis=1)
    cmat = lax.complex(c_re.astype(f32), c_im.astype(f32))
    y = jnp.einsum('bsgp,ghp->bsgh', xs, cmat).real.reshape(Bn, S, S5_WIDTH) + d_skip.astype(f32) * uf
    g = jax.nn.gelu(y)
    out = g * jax.nn.sigmoid(g @ glu_w.astype(f32) + glu_b.astype(f32))
    return out.astype(u.dtype)


def diff_attention(q, k, v, q_g, k_g, lq1, lk1, lq2, lk2, subln_g, lambda_init):
    f32 = jnp.float32
    Bn, S, _ = q.shape
    nb = S // QBLOCK
    q = rms_norm(q.reshape(Bn, S, DIFF_HEADS, 2, DIFF_DH), q_g) * DIFF_DH ** -0.5
    k = rms_norm(k.reshape(Bn, S, DIFF_HEADS, 2, DIFF_DH), k_g)
    v = v.reshape(Bn, S, DIFF_HEADS, DIFF_DV)
    lam = (jnp.exp(jnp.sum(lq1.astype(f32) * lk1.astype(f32)))
           - jnp.exp(jnp.sum(lq2.astype(f32) * lk2.astype(f32))) + lambda_init)
    slopes = jnp.exp2(-8.0 * jnp.arange(1, DIFF_HEADS + 1, dtype=f32) / DIFF_HEADS)
    k_pos = jnp.arange(S)
    qb = q.reshape(Bn, nb, QBLOCK, DIFF_HEADS, 2, DIFF_DH).transpose(1, 0, 2, 3, 4, 5)
    starts = jnp.arange(nb) * QBLOCK

    def block(args):
        q_blk, start = args
        s = jnp.einsum('bqhjd,bkhjd->bhjqk', q_blk, k).astype(f32)
        dist = (start + jnp.arange(QBLOCK))[:, None] - k_pos[None, :]
        bias = -slopes[:, None, None, None] * dist.astype(f32)
        s = jnp.where(dist >= 0, s + bias, -jnp.inf)
        p = jax.nn.softmax(s, axis=-1)
        w = p[:, :, 0] - lam * p[:, :, 1]
        return jnp.einsum('bhqk,bkhe->bqhe', w.astype(v.dtype), v)

    o = lax.map(block, (qb, starts))
    o = o.transpose(1, 0, 2, 3, 4).reshape(Bn, S, DIFF_HEADS, DIFF_DV)
    o = rms_norm(o, subln_g) * (1.0 - lambda_init)
    return o.reshape(Bn, S, DIFF_V_COLS)


def causal_dwconv(h, w, b):
    C = h.shape[-1]
    y = lax.conv_general_dilated(h, w[:, None, :].astype(h.dtype), window_strides=(1,),
                                 padding=[(CONV_W - 1, 0)],
                                 dimension_numbers=('NWC', 'WIO', 'NWC'),
                                 feature_group_count=C)
    return y + b


def conv_gated_mlp(h, w_up, conv_w, conv_b, w_down):
    hh = causal_dwconv(h @ w_up, conv_w, conv_b)
    a, g = jnp.split(hh, 2, axis=-1)
    return (jax.nn.silu(a) * g) @ w_down


def setup_inputs(seed: int = 0) -> dict:
    key = jax.random.key(seed)
    ks = iter(jax.random.split(key, 48))
    L, D, F = DEPTH, D_MODEL, D_FF
    f32 = jnp.float32

    def nrm(shape, scale):
        return jax.random.normal(next(ks), shape, f32) * scale

    def gain(shape):
        return 1.0 + nrm(shape, 0.05)

    x = nrm((BATCH, SEQ, D), 1.0)
    c = nrm((BATCH, D), 1.0)
    n_idx = jnp.arange(S5_STATE, dtype=f32)
    lam_re = -0.5 + nrm((L, S5_GROUPS, S5_STATE), 0.01)
    lam_im = math.pi * n_idx + nrm((L, S5_GROUPS, S5_STATE), 0.01)
    log_dt = jax.random.uniform(next(ks), (L, S5_GROUPS), f32,
                                math.log(S5_DT_MIN), math.log(S5_DT_MAX))
    return {
        'x': x,
        'c': c,
        'ada_w': nrm((L, D, 6 * D), 0.5 * D ** -0.5),
        'ada_b': nrm((L, 6 * D), 0.01),
        'norm1_g': gain((L, D)),
        'w_in': nrm((L, D, IN_COLS), D ** -0.5),
        'gla_gk_w': nrm((L, GLA_LR, GLA_QK_COLS), GLA_LR ** -0.5),
        'gla_gk_b': nrm((L, GLA_QK_COLS), 0.1),
        'gla_norm_g': gain((L, GLA_DV)),
        's5_lambda_re': lam_re,
        's5_lambda_im': lam_im,
        's5_b_re': nrm((L, S5_GROUPS, S5_STATE, S5_GROUP), (2 * S5_GROUP) ** -0.5),
        's5_b_im': nrm((L, S5_GROUPS, S5_STATE, S5_GROUP), (2 * S5_GROUP) ** -0.5),
        's5_c_re': nrm((L, S5_GROUPS, S5_GROUP, S5_STATE), (2 * S5_STATE) ** -0.5),
        's5_c_im': nrm((L, S5_GROUPS, S5_GROUP, S5_STATE), (2 * S5_STATE) ** -0.5),
        's5_d': nrm((L, S5_WIDTH), 0.5),
        's5_log_dt': log_dt,
        's5_glu_w': nrm((L, S5_WIDTH, S5_WIDTH), S5_WIDTH ** -0.5),
        's5_glu_b': nrm((L, S5_WIDTH), 0.01),
        'diff_q_norm_g': gain((L, DIFF_DH)),
        'diff_k_norm_g': gain((L, DIFF_DH)),
        'diff_lambda_q1': nrm((L, DIFF_DH), 0.1),
        'diff_lambda_k1': nrm((L, DIFF_DH), 0.1),
        'diff_lambda_q2': nrm((L, DIFF_DH), 0.1),
        'diff_lambda_k2': nrm((L, DIFF_DH), 0.1),
        'diff_subln_g': gain((L, DIFF_DV)),
        'w_branch_gla': nrm((L, GLA_V_COLS, D), GLA_V_COLS ** -0.5),
        'w_branch_s5': nrm((L, S5_WIDTH, D), S5_WIDTH ** -0.5),
        'w_branch_diff': nrm((L, DIFF_V_COLS, D), DIFF_V_COLS ** -0.5),
        'w_out': nrm((L, D, D), D ** -0.5),
        'norm2_g': gain((L, D)),
        'ffn_w_up': nrm((L, D, 2 * F), D ** -0.5),
        'ffn_conv_w': nrm((L, CONV_W, 2 * F), CONV_W ** -0.5),
        'ffn_conv_b': nrm((L, 2 * F), 0.01),
        'ffn_w_down': nrm((L, F, D), F ** -0.5),
    }


def reference(x, c, ada_w, ada_b, norm1_g, w_in, gla_gk_w, gla_gk_b, gla_norm_g,
              s5_lambda_re, s5_lambda_im, s5_b_re, s5_b_im, s5_c_re, s5_c_im, s5_d, s5_log_dt,
              s5_glu_w, s5_glu_b, diff_q_norm_g, diff_k_norm_g, diff_lambda_q1, diff_lambda_k1,
              diff_lambda_q2, diff_lambda_k2, diff_subln_g, w_branch_gla, w_branch_s5,
              w_branch_diff, w_out, norm2_g, ffn_w_up, ffn_conv_w, ffn_conv_b, ffn_w_down):
    split_pts = _split_points()
    c_act = jax.nn.silu(c)
    for l in range(DEPTH):
        lambda_init = 0.8 - 0.6 * math.exp(-0.3 * l)
        mod = (c_act @ ada_w[l] + ada_b[l])[:, None, :]
        sh1, sc1, gt1, sh2, sc2, gt2 = jnp.split(mod, 6, axis=-1)

        h = rms_norm(x, norm1_g[l]) * (1.0 + sc1) + sh1
        proj = h @ w_in[l]
        gq, gk, gv, gr, glr, su, dq, dk, dv, gates = jnp.split(proj, split_pts, axis=-1)
        o_a = gla_mixer(gq, gk, gv, gr, glr, gla_gk_w[l], gla_gk_b[l], gla_norm_g[l])
        o_b = s5_mixer(su, s5_lambda_re[l], s5_lambda_im[l], s5_b_re[l], s5_b_im[l],
                       s5_c_re[l], s5_c_im[l], s5_d[l], s5_log_dt[l], s5_glu_w[l], s5_glu_b[l])
        o_c = diff_attention(dq, dk, dv, diff_q_norm_g[l], diff_k_norm_g[l],
                             diff_lambda_q1[l], diff_lambda_k1[l], diff_lambda_q2[l],
                             diff_lambda_k2[l], diff_subln_g[l], lambda_init)
        g_a, g_b, g_c = jnp.split(jax.nn.sigmoid(gates), 3, axis=-1)
        merged = (g_a * (o_a @ w_branch_gla[l]) + g_b * (o_b @ w_branch_s5[l])
                  + g_c * (o_c @ w_branch_diff[l]))
        x = x + gt1 * (merged @ w_out[l])

        h = rms_norm(x, norm2_g[l]) * (1.0 + sc2) + sh2
        x = x + gt2 * conv_gated_mlp(h, ffn_w_up[l], ffn_conv_w[l], ffn_conv_b[l], ffn_w_down[l])
    return x
```

```python
import functools
import math

import jax
import jax.numpy as jnp
from jax import lax
from jax.experimental import pallas as pl
from jax.experimental.pallas import tpu as pltpu

F32 = jnp.float32
BF16 = jnp.bfloat16

D_MODEL = 1024
GLA_HEADS = 4
GLA_DK = 64
GLA_DV = 128
GLA_LR = 16
GLA_TAU = 16.0
GLA_CHUNK = 64
GLA_QK = GLA_HEADS * GLA_DK
GLA_V = GLA_HEADS * GLA_DV
GLA_COLS = 2 * GLA_QK + 2 * GLA_V + 128
S5_WIDTH = 512
S5_GROUP = 16
S5_GROUPS = 32
S5_STATE = 64
S5_LANES = S5_GROUPS * S5_STATE
DIFF_HEADS = 4
DIFF_DH = 64
DIFF_DV = 128
D_FF = 2816
EPS = 1e-6
LANE = 128
NEG_BIG = -1e30

TM_IN = 1024
TG = 256
TS = 256
TQ = 512
TM_MERGE = 512
TM_FFN = 1024
TF = 256

_OFF_GLA = 0
_OFF_GLR = 1536
_OFF_SU = 1552
_OFF_DQ = 2064
_OFF_GATES = 3600
_IN_COLS = 6672


def _dot(a, b):
    return jnp.dot(a, b, preferred_element_type=F32)


def _dot_nt(a, b):
    return lax.dot_general(a, b, (((1,), (1,)), ((), ())), preferred_element_type=F32)


def _dot_tn(a, b):
    return lax.dot_general(a, b, (((0,), (0,)), ((), ())), preferred_element_type=F32)


def _norm_mod(x, g, sc, sh):
    ms = jnp.mean(x * x, axis=-1, keepdims=True)
    y = x * lax.rsqrt(ms + EPS) * g
    return y * (1.0 + sc) + sh


def _mod_rows(mod_ref, b, k):
    return mod_ref[0, pl.ds(b, 1), k * D_MODEL:(k + 1) * D_MODEL]


def _mod_kernel(c_ref, w_ref, b_ref, o_ref):
    c = c_ref[...]
    ca = (c * jax.nn.sigmoid(c)).astype(BF16)
    o_ref[0] = _dot(ca, w_ref[0].astype(BF16)) + b_ref[0]


def _modulation(c, ada_w, ada_b):
    depth, d, n = ada_w.shape
    bsz = c.shape[0]
    cp = jnp.zeros((8, d), F32).at[:bsz].set(c)
    tn = 1536
    return pl.pallas_call(
        _mod_kernel,
        out_shape=jax.ShapeDtypeStruct((depth, 8, n), F32),
        grid=(depth, n // tn),
        in_specs=[
            pl.BlockSpec((8, d), lambda l, j: (0, 0)),
            pl.BlockSpec((1, d, tn), lambda l, j: (l, 0, j)),
            pl.BlockSpec((1, 1, tn), lambda l, j: (l, 0, j)),
        ],
        out_specs=pl.BlockSpec((1, 8, tn), lambda l, j: (l, 0, j)),
        name="adaln_mod",
    )(cp, ada_w, ada_b.reshape(depth, 1, n))


def _seg_rmsnorm(a, g128, scale):
    lane = lax.broadcasted_iota(jnp.int32, (1, LANE), 1)
    lo = (lane < DIFF_DH).astype(F32)
    hi = 1.0 - lo
    outs = []
    for hb in range(a.shape[-1] // LANE):
        xh = a[:, hb * LANE:(hb + 1) * LANE]
        sq = xh * xh
        s_lo = jnp.sum(sq * lo, axis=-1, keepdims=True)
        s_hi = jnp.sum(sq * hi, axis=-1, keepdims=True)
        ms = (s_lo * lo + s_hi * hi) * (1.0 / DIFF_DH)
        outs.append(xh * lax.rsqrt(ms + EPS) * g128 * scale)
    return jnp.concatenate(outs, axis=-1)


def _inproj_kernel(x_ref, mod_ref, g_ref, w_ref, *rest, qk_norm):
    if qk_norm:
        qg_ref, kg_ref, o_ref, h_scr = rest
    else:
        o_ref, h_scr = rest
    b = pl.program_id(0)
    j = pl.program_id(2)

    @pl.when(j == 0)
    def _():
        h = _norm_mod(x_ref[0], g_ref[...], _mod_rows(mod_ref, b, 1), _mod_rows(mod_ref, b, 0))
        h_scr[...] = h.astype(BF16)

    acc = _dot(h_scr[...], w_ref[...])
    if not qk_norm:
        o_ref[0] = acc.astype(o_ref.dtype)
    else:
        @pl.when(j == 0)
        def _():
            o_ref[0] = _seg_rmsnorm(acc, qg_ref[...], DIFF_DH ** -0.5).astype(o_ref.dtype)

        @pl.when(j == 1)
        def _():
            o_ref[0] = _seg_rmsnorm(acc, kg_ref[...], 1.0).astype(o_ref.dtype)

        @pl.when(j >= 2)
        def _():
            o_ref[0] = acc.astype(o_ref.dtype)


def _inproj(x, mod, layer, g, w, out_dtype, tn, qk_gains=None, name="inproj"):
    bsz, s, d = x.shape
    n = w.shape[1]
    tm = min(TM_IN, s)
    in_specs = [
        pl.BlockSpec((1, tm, d), lambda b, i, j: (b, i, 0)),
        pl.BlockSpec((1, 8, mod.shape[2]), lambda b, i, j: (layer, 0, 0)),
        pl.BlockSpec((1, d), lambda b, i, j: (0, 0)),
        pl.BlockSpec((d, tn), lambda b, i, j: (0, j)),
    ]
    args = [x, mod, g.reshape(1, d), w]
    if qk_gains is not None:
        in_specs += [pl.BlockSpec((1, LANE), lambda b, i, j: (0, 0))] * 2
        args += [jnp.tile(qk_gains[0], 2).reshape(1, LANE), jnp.tile(qk_gains[1], 2).reshape(1, LANE)]
    return pl.pallas_call(
        functools.partial(_inproj_kernel, qk_norm=qk_gains is not None),
        out_shape=jax.ShapeDtypeStruct((bsz, s, n), out_dtype),
        grid=(bsz, s // tm, n // tn),
        in_specs=in_specs,
        out_specs=pl.BlockSpec((1, tm, tn), lambda b, i, j: (b, i, j)),
        scratch_shapes=[pltpu.VMEM((tm, d), BF16)],
        compiler_params=pltpu.CompilerParams(
            dimension_semantics=("arbitrary", "arbitrary", "arbitrary"),
            vmem_limit_bytes=48 * 1024 * 1024),
        name=name,
    )(*args)


def _gla_kernel(p_ref, gkw_ref, gkb_ref, ng_ref, ltri_ref, lall_ref, o_ref, s_scr, *, nchunk):
    @pl.when(pl.program_id(1) == 0)
    def _():
        s_scr[...] = jnp.zeros_like(s_scr)

    blk = p_ref[0]
    q = blk[:, 0:GLA_QK]
    k = blk[:, GLA_QK:2 * GLA_QK]
    v = blk[:, 2 * GLA_QK:2 * GLA_QK + GLA_V]
    r = blk[:, 2 * GLA_QK + GLA_V:2 * GLA_QK + 2 * GLA_V]
    lr = blk[:, 2 * GLA_QK + 2 * GLA_V:]
    z = _dot(lr.astype(BF16), gkw_ref[...]) + gkb_ref[...]
    la = (jnp.minimum(z, 0.0) - jnp.log1p(jnp.exp(-jnp.abs(z)))) * (1.0 / GLA_TAU)
    la_hi = la.astype(BF16)
    la_lo = (la - la_hi.astype(F32)).astype(BF16)
    ltri = ltri_ref[...]
    lall = lall_ref[...]
    bcum = _dot(ltri, la_hi) + _dot(ltri, la_lo)
    btot = _dot(lall, la_hi) + _dot(lall, la_lo)
    qd = q * (GLA_DK ** -0.5) * jnp.exp(bcum)
    kd = (k * jnp.exp(-bcum)).astype(BF16)
    ke = (k * jnp.exp(btot - bcum)).astype(BF16)
    vb = v.astype(BF16)
    gate = r * jax.nn.sigmoid(r)

    lane_head = lax.broadcasted_iota(jnp.int32, (1, GLA_QK), 1) // GLA_DK
    rowi = lax.broadcasted_iota(jnp.int32, (GLA_QK, GLA_CHUNK), 0) % GLA_CHUNK
    colj = lax.broadcasted_iota(jnp.int32, (GLA_QK, GLA_CHUNK), 1)
    tril = rowi >= colj
    ones_cv = jnp.ones((GLA_CHUNK, GLA_DV), BF16)
    ng = ng_ref[...]

    for c in range(nchunk):
        rows = slice(c * GLA_CHUNK, (c + 1) * GLA_CHUNK)
        qdc = qd[rows]
        qm = jnp.concatenate(
            [jnp.where(lane_head == h, qdc, 0.0).astype(BF16) for h in range(GLA_HEADS)], axis=0)
        att = jnp.where(tril, _dot_nt(qm, kd[rows]), 0.0).astype(BF16)
        state = s_scr[...]
        o_inter = _dot(qm, state.astype(BF16))
        vc = vb[rows]
        outs = []
        for h in range(GLA_HEADS):
            hr = slice(h * GLA_CHUNK, (h + 1) * GLA_CHUNK)
            o_h = _dot(att[hr], vc[:, h * GLA_DV:(h + 1) * GLA_DV]) + o_inter[hr]
            ms = jnp.mean(o_h * o_h, axis=-1, keepdims=True)
            outs.append(o_h * lax.rsqrt(ms + EPS) * ng)
        o = jnp.concatenate(outs, axis=1) * gate[rows]
        o_ref[0, rows, :] = o.astype(o_ref.dtype)

        kv = _dot_tn(ke[rows], vc)
        kvd = jnp.concatenate(
            [kv[h * GLA_DK:(h + 1) * GLA_DK, h * GLA_DV:(h + 1) * GLA_DV] for h in range(GLA_HEADS)],
            axis=0)
        ldec = _dot_tn(la_hi[rows], ones_cv) + _dot_tn(la_lo[rows], ones_cv)
        s_scr[...] = jnp.exp(ldec) * state + kvd


def _gla(p, gk_w, gk_b, norm_g):
    bsz, s, _ = p.shape
    tg = min(TG, s)
    nchunk = tg // GLA_CHUNK
    gkw = jnp.zeros((LANE, GLA_QK), F32).at[:GLA_LR].set(gk_w).astype(BF16)
    ri = jnp.arange(tg)[:, None]
    ci = jnp.arange(tg)[None, :]
    same = (ri // GLA_CHUNK) == (ci // GLA_CHUNK)
    ltri = (same & (ri >= ci)).astype(BF16)
    lall = same.astype(BF16)
    const = lambda shape: pl.BlockSpec(shape, lambda b, t: (0,) * len(shape))
    return pl.pallas_call(
        functools.partial(_gla_kernel, nchunk=nchunk),
        out_shape=jax.ShapeDtypeStruct((bsz, s, GLA_V), BF16),
        grid=(bsz, s // tg),
        in_specs=[
            pl.BlockSpec((1, tg, GLA_COLS), lambda b, t: (b, t, 0)),
            const((LANE, GLA_QK)), const((1, GLA_QK)), const((1, GLA_DV)),
            const((tg, tg)), const((tg, tg)),
        ],
        out_specs=pl.BlockSpec((1, tg, GLA_V), lambda b, t: (b, t, 0)),
        scratch_shapes=[pltpu.VMEM((GLA_QK, GLA_DV), F32)],
        compiler_params=pltpu.CompilerParams(dimension_semantics=("arbitrary", "arbitrary")),
        name="gla",
    )(p, gkw, gk_b.reshape(1, GLA_QK), norm_g.reshape(1, GLA_DV), ltri, lall)


_S5_COLS = S5_LANES // LANE
_S5_CG = 8


def _s5_tables(lam_re, lam_im, b_re, b_im, c_re, c_im, log_dt):
    dt = jnp.exp(log_dt.astype(F32))[:, None]
    ar = lam_re.astype(F32) * dt
    ai = lam_im.astype(F32) * dt

    def power(kk):
        mag = jnp.exp(kk * ar)
        return (mag * jnp.cos(kk * ai)).reshape(-1), (mag * jnp.sin(kk * ai)).reshape(-1)

    lbr, lbi = jnp.exp(ar) * jnp.cos(ai), jnp.exp(ar) * jnp.sin(ai)
    den = lam_re * lam_re + lam_im * lam_im
    nr = ((lbr - 1.0) * lam_re + lbi * lam_im) / den
    ni = (lbi * lam_re - (lbr - 1.0) * lam_im) / den
    bbr = nr[..., None] * b_re - ni[..., None] * b_im
    bbi = nr[..., None] * b_im + ni[..., None] * b_re
    eye = jnp.eye(S5_GROUPS, dtype=F32)
    bfull = jnp.concatenate([
        jnp.einsum('gph,gk->ghkp', bbr, eye).reshape(S5_WIDTH, S5_LANES),
        jnp.einsum('gph,gk->ghkp', bbi, eye).reshape(S5_WIDTH, S5_LANES)], axis=1)
    cfull = jnp.concatenate([
        jnp.einsum('ghp,gk->gpkh', c_re, eye).reshape(S5_LANES, S5_WIDTH),
        jnp.einsum('ghp,gk->gpkh', -c_im, eye).reshape(S5_LANES, S5_WIDTH)], axis=0)
    bt, ct = [], []
    for t in range(16):
        half = (t % 8) // 4
        bt.append(bfull[half * 256:(half + 1) * 256, t * 256:(t + 1) * 256])
        ct.append(cfull[t * 256:(t + 1) * 256, half * 256:(half + 1) * 256])
    bt = jnp.stack(bt).astype(BF16)
    ct = jnp.stack(ct).astype(BF16)
    rows = jnp.arange(8)[:, None]
    tabs = []
    for dstep in (1, 2, 4):
        pr, pi = power(float(dstep))
        keep = (rows >= dstep).astype(F32)
        tabs += [keep * pr[None, :], keep * pi[None, :]]
    prs, pis = zip(*[power(float(i + 1)) for i in range(8)])
    tabs += [jnp.stack(prs), jnp.stack(pis)]
    return bt, ct, jnp.concatenate(tabs, axis=1)


def _s5_kernel(u_ref, bt_ref, ct_ref, tab_ref, d_ref, gw_ref, gb_ref, o_ref, bu_scr, car_scr, *, ts):
    @pl.when(pl.program_id(1) == 0)
    def _():
        car_scr[...] = jnp.zeros_like(car_scr)

    u = u_ref[0]
    for t in range(16):
        half = (t % 8) // 4
        bu_scr[:, t * 256:(t + 1) * 256] = _dot(u[:, half * 256:(half + 1) * 256], bt_ref[t])

    def tab(idx, j):
        return tab_ref[:, idx * S5_LANES + j * LANE: idx * S5_LANES + (j + 1) * LANE]

    for j0 in range(0, _S5_COLS, _S5_CG):
        cols = list(range(j0, j0 + _S5_CG))
        re_sl = [slice(j * LANE, (j + 1) * LANE) for j in cols]
        im_sl = [slice(S5_LANES + j * LANE, S5_LANES + (j + 1) * LANE) for j in cols]
        init = tuple(car_scr[:, sl] for sl in re_sl) + tuple(car_scr[:, sl] for sl in im_sl)

        def body(kk, carry, cols=cols, re_sl=re_sl, im_sl=im_sl):
            r0 = pl.multiple_of(kk * 8, 8)
            new = [None] * (2 * len(cols))
            for n, j in enumerate(cols):
                cr, ci = carry[n], carry[len(cols) + n]
                vr = bu_scr[pl.ds(r0, 8), re_sl[n]]
                vi = bu_scr[pl.ds(r0, 8), im_sl[n]]
                for si, dstep in enumerate((1, 2, 4)):
                    lr_, li_ = tab(2 * si, j), tab(2 * si + 1, j)
                    sr = pltpu.roll(vr, dstep, 0)
                    sm = pltpu.roll(vi, dstep, 0)
                    vr, vi = vr + lr_ * sr - li_ * sm, vi + lr_ * sm + li_ * sr
                pr_, pi_ = tab(6, j), tab(7, j)
                vr, vi = vr + pr_ * cr - pi_ * ci, vi + pr_ * ci + pi_ * cr
                bu_scr[pl.ds(r0, 8), re_sl[n]] = vr
                bu_scr[pl.ds(r0, 8), im_sl[n]] = vi
                new[n] = jnp.broadcast_to(vr[7:8, :], (8, LANE))
                new[len(cols) + n] = jnp.broadcast_to(vi[7:8, :], (8, LANE))
            return tuple(new)

        fin = lax.fori_loop(0, ts // 8, body, init)
        for n in range(len(cols)):
            car_scr[:, re_sl[n]] = fin[n]
            car_scr[:, im_sl[n]] = fin[len(cols) + n]

    ys = []
    for half in range(2):
        acc = None
        for t in [half * 4 + i for i in range(4)] + [8 + half * 4 + i for i in range(4)]:
            part = _dot(bu_scr[:, t * 256:(t + 1) * 256].astype(BF16), ct_ref[t])
            acc = part if acc is None else acc + part
        ys.append(acc)
    y = jnp.concatenate(ys, axis=1) + d_ref[...] * u.astype(F32)
    g = 0.5 * y * (1.0 + jnp.tanh(math.sqrt(2.0 / math.pi) * (y + 0.044715 * (y * y * y))))
    zz = _dot(g.astype(BF16), gw_ref[...]) + gb_ref[...]
    o_ref[0] = (g * jax.nn.sigmoid(zz)).astype(o_ref.dtype)


def _s5(qkvu, bt, ct, tab, d_skip, glu_w, glu_b):
    bsz, s, _ = qkvu.shape
    ts = min(TS, s)
    const = lambda shape: pl.BlockSpec(shape, lambda b, t: (0,) * len(shape))
    return pl.pallas_call(
        functools.partial(_s5_kernel, ts=ts),
        out_shape=jax.ShapeDtypeStruct((bsz, s, S5_WIDTH), BF16),
        grid=(bsz, s // ts),
        in_specs=[
            pl.BlockSpec((1, ts, S5_WIDTH), lambda b, t: (b, t, 3)),
            const((16, 256, 256)), const((16, 256, 256)), const((8, 8 * S5_LANES)),
            const((1, S5_WIDTH)), const((S5_WIDTH, S5_WIDTH)), const((1, S5_WIDTH)),
        ],
        out_specs=pl.BlockSpec((1, ts, S5_WIDTH), lambda b, t: (b, t, 0)),
        scratch_shapes=[pltpu.VMEM((ts, 2 * S5_LANES), F32), pltpu.VMEM((8, 2 * S5_LANES), F32)],
        compiler_params=pltpu.CompilerParams(
            dimension_semantics=("arbitrary", "arbitrary"), vmem_limit_bytes=48 * 1024 * 1024),
        name="s5",
    )(qkvu, bt, ct, tab, d_skip.reshape(1, S5_WIDTH), glu_w.astype(BF16), glu_b.reshape(1, S5_WIDTH))


def _attn_kernel(q_ref, k_ref, v_ref, slope_ref, lam_ref, sg_ref, o_ref,
                 m_scr, l_scr, acc_scr, *, tq, lambda_init):
    qi = pl.program_id(2)
    slope = slope_ref[0, 0:1, 0:1]
    q = q_ref[0]
    lane = lax.broadcasted_iota(jnp.int32, (1, LANE), 1)
    qh = (jnp.where(lane < DIFF_DH, q, jnp.zeros_like(q)), jnp.where(lane >= DIFF_DH, q, jnp.zeros_like(q)))
    il = lax.broadcasted_iota(jnp.int32, (tq, tq), 0)
    jl = lax.broadcasted_iota(jnp.int32, (tq, tq), 1)
    rel = (jl - il).astype(F32) * slope

    m_scr[...] = jnp.full_like(m_scr, NEG_BIG)
    l_scr[...] = jnp.zeros_like(l_scr)
    acc_scr[...] = jnp.zeros_like(acc_scr)

    def block(kj, masked):
        start = pl.multiple_of(kj * tq, tq)
        kb = k_ref[0, pl.ds(start, tq), :]
        vb = v_ref[0, pl.ds(start, tq), :]
        shift = slope * ((kj - qi) * tq).astype(F32)
        for half in range(2):
            t = _dot_nt(qh[half], kb) + rel
            if masked:
                t = jnp.where(jl <= il, t, NEG_BIG)
            m_old = m_scr[half] - shift
            m_new = jnp.maximum(m_old, jnp.max(t, axis=-1, keepdims=True))
            alpha = jnp.exp(m_old - m_new)
            p = jnp.exp(t - m_new)
            l_scr[half] = alpha * l_scr[half] + jnp.sum(p, axis=-1, keepdims=True)
            acc_scr[half] = alpha * acc_scr[half] + _dot(p.astype(BF16), vb)
            m_scr[half] = m_new + shift

    def body(kj, carry):
        block(kj, False)
        return carry

    lax.fori_loop(0, qi, body, 0)
    block(qi, True)

    lam = (jnp.exp(jnp.sum(lam_ref[0:1, :] * lam_ref[1:2, :], axis=-1, keepdims=True))
           - jnp.exp(jnp.sum(lam_ref[2:3, :] * lam_ref[3:4, :], axis=-1, keepdims=True)) + lambda_init)
    o = acc_scr[0] / l_scr[0] - lam * (acc_scr[1] / l_scr[1])
    ms = jnp.mean(o * o, axis=-1, keepdims=True)
    o = o * lax.rsqrt(ms + EPS) * sg_ref[...] * (1.0 - lambda_init)
    o_ref[0] = o.astype(o_ref.dtype)


def _diff_attention(qkvu, lam_rows, subln_g, lambda_init):
    bsz, s, _ = qkvu.shape
    tq = min(TQ, s)
    slopes = jnp.exp2(-8.0 * jnp.arange(1, DIFF_HEADS + 1, dtype=F32) / DIFF_HEADS)
    slope_arr = jnp.broadcast_to(slopes[:, None, None], (DIFF_HEADS, 8, LANE))
    return pl.pallas_call(
        functools.partial(_attn_kernel, tq=tq, lambda_init=lambda_init),
        out_shape=jax.ShapeDtypeStruct((bsz, s, DIFF_HEADS * DIFF_DV), BF16),
        grid=(bsz, DIFF_HEADS, s // tq),
        in_specs=[
            pl.BlockSpec((1, tq, LANE), lambda b, h, i: (b, i, h)),
            pl.BlockSpec((1, s, LANE), lambda b, h, i: (b, 0, DIFF_HEADS + h)),
            pl.BlockSpec((1, s, LANE), lambda b, h, i: (b, 0, 2 * DIFF_HEADS + h)),
            pl.BlockSpec((1, 8, LANE), lambda b, h, i: (h, 0, 0)),
            pl.BlockSpec((8, LANE), lambda b, h, i: (0, 0)),
            pl.BlockSpec((1, LANE), lambda b, h, i: (0, 0)),
        ],
        out_specs=pl.BlockSpec((1, tq, LANE), lambda b, h, i: (b, i, h)),
        scratch_shapes=[
            pltpu.VMEM((2, tq, 1), F32), pltpu.VMEM((2, tq, 1), F32), pltpu.VMEM((2, tq, DIFF_DV), F32)],
        compiler_params=pltpu.CompilerParams(
            dimension_semantics=("arbitrary", "arbitrary", "arbitrary"),
            vmem_limit_bytes=48 * 1024 * 1024),
        name="diff_attn",
    )(qkvu, qkvu, qkvu, slope_arr, lam_rows, subln_g.reshape(1, DIFF_DV))


def _merge_kernel(x_ref, mod_ref, g_ref, wg_ref, oa_ref, ob_ref, oc_ref, wa_ref, wb_ref, wc_ref,
                  wo_ref, o_ref):
    b = pl.program_id(0)
    x = x_ref[0]
    h = _norm_mod(x, g_ref[...], _mod_rows(mod_ref, b, 1), _mod_rows(mod_ref, b, 0)).astype(BF16)
    merged = None
    for n, (ob, wb) in enumerate(((oa_ref, wa_ref), (ob_ref, wb_ref), (oc_ref, wc_ref))):
        gate = jax.nn.sigmoid(_dot(h, wg_ref[:, n * D_MODEL:(n + 1) * D_MODEL]))
        term = gate * _dot(ob[0], wb[...])
        merged = term if merged is None else merged + term
    y = _dot(merged.astype(BF16), wo_ref[...])
    o_ref[0] = x + _mod_rows(mod_ref, b, 2) * y


def _merge(x, mod, layer, g, w_gate, o_a, o_b, o_c, w_a, w_b, w_c, w_out):
    bsz, s, d = x.shape
    tm = min(TM_MERGE, s)
    const = lambda shape: pl.BlockSpec(shape, lambda b, i: (0,) * len(shape))
    tok = lambda n: pl.BlockSpec((1, tm, n), lambda b, i: (b, i, 0))
    return pl.pallas_call(
        _merge_kernel,
        out_shape=jax.ShapeDtypeStruct((bsz, s, d), F32),
        grid=(bsz, s // tm),
        in_specs=[
            tok(d),
            pl.BlockSpec((1, 8, mod.shape[2]), lambda b, i: (layer, 0, 0)),
            const((1, d)), const((d, 3 * d)),
            tok(GLA_V), tok(S5_WIDTH), tok(DIFF_HEADS * DIFF_DV),
            const((GLA_V, d)), const((S5_WIDTH, d)), const((DIFF_HEADS * DIFF_DV, d)), const((d, d)),
        ],
        out_specs=tok(d),
        compiler_params=pltpu.CompilerParams(
            dimension_semantics=("arbitrary", "arbitrary"), vmem_limit_bytes=56 * 1024 * 1024),
        name="merge",
    )(x, mod, g.reshape(1, d), w_gate, o_a, o_b, o_c, w_a, w_b, w_c, w_out)


_HALO = 16


def _ffn_kernel(x_ref, mod_ref, g_ref, wa_ref, wg_ref, cwa_ref, cwg_ref, cba_ref, cbg_ref, wd_ref,
                o_ref, h_scr, halo_scr, acc_scr, *, tm):
    b = pl.program_id(0)
    i = pl.program_id(1)
    j = pl.program_id(2)

    @pl.when(j == 0)
    def _():
        @pl.when(i == 0)
        def _():
            halo_scr[...] = jnp.zeros_like(halo_scr)

        h = _norm_mod(x_ref[0], g_ref[...], _mod_rows(mod_ref, b, 4), _mod_rows(mod_ref, b, 3)).astype(BF16)
        h_scr[0:_HALO, :] = halo_scr[...]
        h_scr[_HALO:, :] = h
        halo_scr[...] = h[tm - _HALO:, :]
        acc_scr[...] = jnp.zeros_like(acc_scr)

    hh = h_scr[...]

    def conv(up, cw_ref, cb_ref):
        m1 = pltpu.roll(up, 1, 0)
        m2 = pltpu.roll(up, 2, 0)
        y = cw_ref[0:1, :] * m2 + cw_ref[1:2, :] * m1 + cw_ref[2:3, :] * up + cb_ref[...]
        return y[_HALO:, :]

    a = conv(_dot(hh, wa_ref[...]), cwa_ref, cba_ref)
    gg = conv(_dot(hh, wg_ref[...]), cwg_ref, cbg_ref)
    act = (a * jax.nn.sigmoid(a) * gg).astype(BF16)
    acc_scr[...] += _dot(act, wd_ref[...])

    @pl.when(j == pl.num_programs(2) - 1)
    def _():
        o_ref[0] = x_ref[0] + _mod_rows(mod_ref, b, 5) * acc_scr[...]


def _ffn(x, mod, layer, g, w_up, conv_w, conv_b, w_down):
    bsz, s, d = x.shape
    f = w_down.shape[0]
    tm = min(TM_FFN, s)
    tf = TF
    nf = f // tf
    cb = conv_b.reshape(1, 2 * f)
    return pl.pallas_call(
        functools.partial(_ffn_kernel, tm=tm),
        out_shape=jax.ShapeDtypeStruct((bsz, s, d), F32),
        grid=(bsz, s // tm, nf),
        in_specs=[
            pl.BlockSpec((1, tm, d), lambda b, i, j: (b, i, 0)),
            pl.BlockSpec((1, 8, mod.shape[2]), lambda b, i, j: (layer, 0, 0)),
            pl.BlockSpec((1, d), lambda b, i, j: (0, 0)),
            pl.BlockSpec((d, tf), lambda b, i, j: (0, j)),
            pl.BlockSpec((d, tf), lambda b, i, j: (0, nf + j)),
            pl.BlockSpec((3, tf), lambda b, i, j: (0, j)),
            pl.BlockSpec((3, tf), lambda b, i, j: (0, nf + j)),
            pl.BlockSpec((1, tf), lambda b, i, j: (0, j)),
            pl.BlockSpec((1, tf), lambda b, i, j: (0, nf + j)),
            pl.BlockSpec((tf, d), lambda b, i, j: (j, 0)),
        ],
        out_specs=pl.BlockSpec((1, tm, d), lambda b, i, j: (b, i, 0)),
        scratch_shapes=[
            pltpu.VMEM((tm + _HALO, d), BF16), pltpu.VMEM((_HALO, d), BF16), pltpu.VMEM((tm, d), F32)],
        compiler_params=pltpu.CompilerParams(
            dimension_semantics=("arbitrary", "arbitrary", "arbitrary"),
            vmem_limit_bytes=48 * 1024 * 1024),
        name="ffn",
    )(x, mod, g.reshape(1, d), w_up, w_up, conv_w, conv_w, cb, cb, w_down)


def kernel(x, c, ada_w, ada_b, norm1_g, w_in, gla_gk_w, gla_gk_b, gla_norm_g, s5_lambda_re, s5_lambda_im, s5_b_re, s5_b_im, s5_c_re, s5_c_im, s5_d, s5_log_dt, s5_glu_w, s5_glu_b, diff_q_norm_g, diff_k_norm_g, diff_lambda_q1, diff_lambda_k1, diff_lambda_q2, diff_lambda_k2, diff_subln_g, w_branch_gla, w_branch_s5, w_branch_diff, w_out, norm2_g, ffn_w_up, ffn_conv_w, ffn_conv_b, ffn_w_down):
    depth = ada_w.shape[0]
    mod = _modulation(c, ada_w, ada_b)
    for l in range(depth):
        lambda_init = 0.8 - 0.6 * math.exp(-0.3 * l)
        wl = w_in[l]
        w_gla = jnp.concatenate(
            [wl[:, _OFF_GLA:_OFF_GLR], wl[:, _OFF_GLR:_OFF_SU],
             jnp.zeros((D_MODEL, LANE - GLA_LR), F32)], axis=1).astype(BF16)
        w_qkvu = jnp.concatenate([wl[:, _OFF_DQ:_OFF_GATES], wl[:, _OFF_SU:_OFF_DQ]], axis=1).astype(BF16)
        w_gate = wl[:, _OFF_GATES:].astype(BF16)

        p_gla = _inproj(x, mod, l, norm1_g[l], w_gla, F32, GLA_COLS, name="inproj_gla")
        qkvu = _inproj(x, mod, l, norm1_g[l], w_qkvu, BF16, 512,
                       qk_gains=(diff_q_norm_g[l], diff_k_norm_g[l]), name="inproj_qkvu")

        o_a = _gla(p_gla, gla_gk_w[l], gla_gk_b[l], gla_norm_g[l])
        bt, ct, tab = _s5_tables(s5_lambda_re[l], s5_lambda_im[l], s5_b_re[l], s5_b_im[l],
                                 s5_c_re[l], s5_c_im[l], s5_log_dt[l])
        o_b = _s5(qkvu, bt, ct, tab, s5_d[l], s5_glu_w[l], s5_glu_b[l])
        lam_rows = jnp.zeros((8, LANE), F32).at[0:4, 0:DIFF_DH].set(
            jnp.stack([diff_lambda_q1[l], diff_lambda_k1[l], diff_lambda_q2[l], diff_lambda_k2[l]]))
        o_c = _diff_attention(qkvu, lam_rows, diff_subln_g[l], lambda_init)

        x = _merge(x, mod, l, norm1_g[l], w_gate, o_a, o_b, o_c,
                   w_branch_gla[l].astype(BF16), w_branch_s5[l].astype(BF16),
                   w_branch_diff[l].astype(BF16), w_out[l].astype(BF16))
        x = _ffn(x, mod, l, norm2_g[l], ffn_w_up[l].astype(BF16), ffn_conv_w[l], ffn_conv_b[l],
                 ffn_w_down[l].astype(BF16))
    return x
```

```python
import functools
import math

import jax
import jax.numpy as jnp
import numpy as np
from jax import lax
from jax.experimental import pallas as pl
from jax.experimental.pallas import tpu as pltpu

F32 = jnp.float32
BF16 = jnp.bfloat16

D_MODEL = 1024
GLA_HEADS = 4
GLA_DK = 64
GLA_DV = 128
GLA_LR = 16
GLA_TAU = 16.0
GLA_CHUNK = 64
GLA_QK = GLA_HEADS * GLA_DK
GLA_V = GLA_HEADS * GLA_DV
GLA_COLS = 2 * GLA_QK + 2 * GLA_V + 128
S5_WIDTH = 512
S5_GROUP = 16
S5_GROUPS = 32
S5_STATE = 64
S5_LANES = S5_GROUPS * S5_STATE
DIFF_HEADS = 4
DIFF_DH = 64
DIFF_DV = 128
D_FF = 2816
EPS = 1e-6
LANE = 128
NEG_BIG = -1e30

TM_IN = 1024
TG = 256
TS = 256
TQ = 1024
TK = 512
TM_MERGE = 512
TM_FFN = 512
TF = 256

_OFF_GLA = 0
_OFF_GLR = 1536
_OFF_SU = 1552
_OFF_DQ = 2064
_OFF_GATES = 3600
_IN_COLS = 6672


def _dot(a, b):
    return jnp.dot(a, b, preferred_element_type=F32)


def _dot_nt(a, b):
    return lax.dot_general(a, b, (((1,), (1,)), ((), ())), preferred_element_type=F32)


def _dot_tn(a, b):
    return lax.dot_general(a, b, (((0,), (0,)), ((), ())), preferred_element_type=F32)


def _norm_mod(x, g, sc, sh):
    ms = jnp.mean(x * x, axis=-1, keepdims=True)
    y = x * lax.rsqrt(ms + EPS) * g
    return y * (1.0 + sc) + sh


def _mod_rows(mod_ref, b, k):
    return mod_ref[0, pl.ds(b, 1), k * D_MODEL:(k + 1) * D_MODEL]


def _mod_kernel(c_ref, w_ref, b_ref, o_ref):
    c = c_ref[...]
    ca = (c * jax.nn.sigmoid(c)).astype(BF16)
    o_ref[0] = _dot(ca, w_ref[0].astype(BF16)) + b_ref[0]


def _modulation(c, ada_w, ada_b):
    depth, d, n = ada_w.shape
    bsz = c.shape[0]
    cp = jnp.zeros((8, d), F32).at[:bsz].set(c)
    tn = 1536
    return pl.pallas_call(
        _mod_kernel,
        out_shape=jax.ShapeDtypeStruct((depth, 8, n), F32),
        grid=(depth, n // tn),
        in_specs=[
            pl.BlockSpec((8, d), lambda l, j: (0, 0)),
            pl.BlockSpec((1, d, tn), lambda l, j: (l, 0, j)),
            pl.BlockSpec((1, 1, tn), lambda l, j: (l, 0, j)),
        ],
        out_specs=pl.BlockSpec((1, 8, tn), lambda l, j: (l, 0, j)),
        name="adaln_mod",
    )(cp, ada_w, ada_b.reshape(depth, 1, n))


def _seg_rmsnorm(a, g128, scale):
    lane = lax.broadcasted_iota(jnp.int32, (1, LANE), 1)
    lo = (lane < DIFF_DH).astype(F32)
    hi = 1.0 - lo
    outs = []
    for hb in range(a.shape[-1] // LANE):
        xh = a[:, hb * LANE:(hb + 1) * LANE]
        sq = xh * xh
        s_lo = jnp.sum(sq * lo, axis=-1, keepdims=True)
        s_hi = jnp.sum(sq * hi, axis=-1, keepdims=True)
        ms = (s_lo * lo + s_hi * hi) * (1.0 / DIFF_DH)
        outs.append(xh * lax.rsqrt(ms + EPS) * g128 * scale)
    return jnp.concatenate(outs, axis=-1)


def _inproj_gla_kernel(x_ref, mod_ref, g_ref, w_ref, o_ref):
    b = pl.program_id(0)
    h = _norm_mod(x_ref[0], g_ref[...], _mod_rows(mod_ref, b, 1), _mod_rows(mod_ref, b, 0))
    o_ref[0] = _dot(h.astype(BF16), w_ref[...])


def _inproj_gla(x, mod, layer, g, w):
    bsz, s, d = x.shape
    n = w.shape[1]
    tm = min(TM_IN, s)
    return pl.pallas_call(
        _inproj_gla_kernel,
        out_shape=jax.ShapeDtypeStruct((bsz, s, n), F32),
        grid=(bsz, s // tm),
        in_specs=[
            pl.BlockSpec((1, tm, d), lambda b, i: (b, i, 0)),
            pl.BlockSpec((1, 8, mod.shape[2]), lambda b, i: (layer, 0, 0)),
            pl.BlockSpec((1, d), lambda b, i: (0, 0)),
            pl.BlockSpec((d, n), lambda b, i: (0, 0)),
        ],
        out_specs=pl.BlockSpec((1, tm, n), lambda b, i: (b, i, 0)),
        compiler_params=pltpu.CompilerParams(
            dimension_semantics=("arbitrary", "arbitrary"), vmem_limit_bytes=48 * 1024 * 1024),
        name="inproj_gla",
    )(x, mod, g.reshape(1, d), w)


_QKU_TILE = 512


def _inproj_qkuv_kernel(x_ref, mod_ref, g_ref, w_ref, wvt_ref, qg_ref, kg_ref, o_ref, vt_ref, h_scr):
    b = pl.program_id(0)
    j = pl.program_id(2)

    @pl.when(j == 0)
    def _():
        h = _norm_mod(x_ref[0], g_ref[...], _mod_rows(mod_ref, b, 1), _mod_rows(mod_ref, b, 0))
        h_scr[...] = h.astype(BF16)

    @pl.when(j == 0)
    def _():
        acc = _dot(h_scr[...], w_ref[...])
        o_ref[0] = _seg_rmsnorm(acc, qg_ref[...], DIFF_DH ** -0.5 * math.log2(math.e)).astype(o_ref.dtype)

    @pl.when(j == 1)
    def _():
        acc = _dot(h_scr[...], w_ref[...])
        o_ref[0] = _seg_rmsnorm(acc, kg_ref[...], 1.0).astype(o_ref.dtype)

    @pl.when(j == 2)
    def _():
        o_ref[0] = _dot(h_scr[...], w_ref[...]).astype(o_ref.dtype)

    @pl.when(j == 3)
    def _():
        vt = _dot_nt(wvt_ref[...], h_scr[...]).astype(vt_ref.dtype)
        tk = vt_ref.shape[3]
        for n in range(vt_ref.shape[1]):
            vt_ref[0, n] = vt[:, n * tk:(n + 1) * tk]


def _inproj_qkuv(x, mod, layer, g, w_qku, w_vt, q_gain, k_gain):
    bsz, s, d = x.shape
    tm = min(TM_IN, s)
    tq = min(TK, s)
    tn = _QKU_TILE
    nv = w_vt.shape[0]
    return pl.pallas_call(
        _inproj_qkuv_kernel,
        out_shape=(jax.ShapeDtypeStruct((bsz, s, 3 * tn), BF16),
                   jax.ShapeDtypeStruct((bsz, s // tq, nv, tq), BF16)),
        grid=(bsz, s // tm, 4),
        in_specs=[
            pl.BlockSpec((1, tm, d), lambda b, i, j: (b, i, 0)),
            pl.BlockSpec((1, 8, mod.shape[2]), lambda b, i, j: (layer, 0, 0)),
            pl.BlockSpec((1, d), lambda b, i, j: (0, 0)),
            pl.BlockSpec((d, tn), lambda b, i, j: (0, jnp.minimum(j, 2))),
            pl.BlockSpec((nv, d), lambda b, i, j: (0, 0)),
            pl.BlockSpec((1, LANE), lambda b, i, j: (0, 0)),
            pl.BlockSpec((1, LANE), lambda b, i, j: (0, 0)),
        ],
        out_specs=(
            pl.BlockSpec((1, tm, tn), lambda b, i, j: (b, i, jnp.minimum(j, 2))),
            pl.BlockSpec((1, tm // tq, nv, tq), lambda b, i, j: (b, i, 0, 0)),
        ),
        scratch_shapes=[pltpu.VMEM((tm, d), BF16)],
        compiler_params=pltpu.CompilerParams(
            dimension_semantics=("arbitrary", "arbitrary", "arbitrary"),
            vmem_limit_bytes=48 * 1024 * 1024),
        name="inproj_qkuv",
    )(x, mod, g.reshape(1, d), w_qku, w_vt,
      jnp.tile(q_gain, 2).reshape(1, LANE), jnp.tile(k_gain, 2).reshape(1, LANE))


def _gla_kernel(p_ref, gkw_ref, gkb_ref, ng_ref, ltri_ref, lall_ref, o_ref, s_scr, *, nchunk):
    @pl.when(pl.program_id(1) == 0)
    def _():
        s_scr[...] = jnp.zeros_like(s_scr)

    blk = p_ref[0]
    q = blk[:, 0:GLA_QK]
    k = blk[:, GLA_QK:2 * GLA_QK]
    v = blk[:, 2 * GLA_QK:2 * GLA_QK + GLA_V]
    r = blk[:, 2 * GLA_QK + GLA_V:2 * GLA_QK + 2 * GLA_V]
    lr = blk[:, 2 * GLA_QK + 2 * GLA_V:]
    z = _dot(lr.astype(BF16), gkw_ref[...]) + gkb_ref[...]
    la = (jnp.minimum(z, 0.0) - jnp.log1p(jnp.exp(-jnp.abs(z)))) * (1.0 / GLA_TAU)
    la_hi = la.astype(BF16)
    la_lo = (la - la_hi.astype(F32)).astype(BF16)
    ltri = ltri_ref[...]
    lall = lall_ref[...]
    bcum = _dot(ltri, la_hi) + _dot(ltri, la_lo)
    btot = _dot(lall, la_hi) + _dot(lall, la_lo)
    qd = q * (GLA_DK ** -0.5) * jnp.exp(bcum)
    kd = (k * jnp.exp(-bcum)).astype(BF16)
    ke = (k * jnp.exp(btot - bcum)).astype(BF16)
    vb = v.astype(BF16)
    gate = r * jax.nn.sigmoid(r)

    lane_head = lax.broadcasted_iota(jnp.int32, (1, GLA_QK), 1) // GLA_DK
    rowi = lax.broadcasted_iota(jnp.int32, (GLA_QK, GLA_CHUNK), 0) % GLA_CHUNK
    colj = lax.broadcasted_iota(jnp.int32, (GLA_QK, GLA_CHUNK), 1)
    tril = rowi >= colj
    ones_cv = jnp.ones((GLA_CHUNK, GLA_DV), BF16)
    ng = ng_ref[...]

    for c in range(nchunk):
        rows = slice(c * GLA_CHUNK, (c + 1) * GLA_CHUNK)
        qdc = qd[rows]
        qm = jnp.concatenate(
            [jnp.where(lane_head == h, qdc, 0.0).astype(BF16) for h in range(GLA_HEADS)], axis=0)
        att = jnp.where(tril, _dot_nt(qm, kd[rows]), 0.0).astype(BF16)
        state = s_scr[...]
        o_inter = _dot(qm, state.astype(BF16))
        vc = vb[rows]
        outs = []
        for h in range(GLA_HEADS):
            hr = slice(h * GLA_CHUNK, (h + 1) * GLA_CHUNK)
            o_h = _dot(att[hr], vc[:, h * GLA_DV:(h + 1) * GLA_DV]) + o_inter[hr]
            ms = jnp.mean(o_h * o_h, axis=-1, keepdims=True)
            outs.append(o_h * lax.rsqrt(ms + EPS) * ng)
        o = jnp.concatenate(outs, axis=1) * gate[rows]
        o_ref[0, rows, :] = o.astype(o_ref.dtype)

        kv = _dot_tn(ke[rows], vc)
        kvd = jnp.concatenate(
            [kv[h * GLA_DK:(h + 1) * GLA_DK, h * GLA_DV:(h + 1) * GLA_DV] for h in range(GLA_HEADS)],
            axis=0)
        ldec = _dot_tn(la_hi[rows], ones_cv) + _dot_tn(la_lo[rows], ones_cv)
        s_scr[...] = jnp.exp(ldec) * state + kvd


def _gla(p, gk_w, gk_b, norm_g):
    bsz, s, _ = p.shape
    tg = min(TG, s)
    nchunk = tg // GLA_CHUNK
    gkw = jnp.zeros((LANE, GLA_QK), F32).at[:GLA_LR].set(gk_w).astype(BF16)
    ri = jnp.arange(tg)[:, None]
    ci = jnp.arange(tg)[None, :]
    same = (ri // GLA_CHUNK) == (ci // GLA_CHUNK)
    ltri = (same & (ri >= ci)).astype(BF16)
    lall = same.astype(BF16)
    const = lambda shape: pl.BlockSpec(shape, lambda b, t: (0,) * len(shape))
    return pl.pallas_call(
        functools.partial(_gla_kernel, nchunk=nchunk),
        out_shape=jax.ShapeDtypeStruct((bsz, s, GLA_V), BF16),
        grid=(bsz, s // tg),
        in_specs=[
            pl.BlockSpec((1, tg, GLA_COLS), lambda b, t: (b, t, 0)),
            const((LANE, GLA_QK)), const((1, GLA_QK)), const((1, GLA_DV)),
            const((tg, tg)), const((tg, tg)),
        ],
        out_specs=pl.BlockSpec((1, tg, GLA_V), lambda b, t: (b, t, 0)),
        scratch_shapes=[pltpu.VMEM((GLA_QK, GLA_DV), F32)],
        compiler_params=pltpu.CompilerParams(dimension_semantics=("arbitrary", "arbitrary")),
        name="gla",
    )(p, gkw, gk_b.reshape(1, GLA_QK), norm_g.reshape(1, GLA_DV), ltri, lall)


_S5_COLS = S5_LANES // LANE
_S5_CG = 8


def _s5_tables(lam_re, lam_im, b_re, b_im, c_re, c_im, log_dt):
    dt = jnp.exp(log_dt.astype(F32))[:, None]
    ar = lam_re.astype(F32) * dt
    ai = lam_im.astype(F32) * dt

    def power(kk):
        mag = jnp.exp(kk * ar)
        return (mag * jnp.cos(kk * ai)).reshape(-1), (mag * jnp.sin(kk * ai)).reshape(-1)

    lbr, lbi = jnp.exp(ar) * jnp.cos(ai), jnp.exp(ar) * jnp.sin(ai)
    den = lam_re * lam_re + lam_im * lam_im
    nr = ((lbr - 1.0) * lam_re + lbi * lam_im) / den
    ni = (lbi * lam_re - (lbr - 1.0) * lam_im) / den
    bbr = nr[..., None] * b_re - ni[..., None] * b_im
    bbi = nr[..., None] * b_im + ni[..., None] * b_re
    eye = jnp.eye(S5_GROUPS, dtype=F32)
    bfull = jnp.concatenate([
        jnp.einsum('gph,gk->ghkp', bbr, eye).reshape(S5_WIDTH, S5_LANES),
        jnp.einsum('gph,gk->ghkp', bbi, eye).reshape(S5_WIDTH, S5_LANES)], axis=1)
    cfull = jnp.concatenate([
        jnp.einsum('ghp,gk->gpkh', c_re, eye).reshape(S5_LANES, S5_WIDTH),
        jnp.einsum('ghp,gk->gpkh', -c_im, eye).reshape(S5_LANES, S5_WIDTH)], axis=0)
    bt, ct = [], []
    for t in range(16):
        half = (t % 8) // 4
        bt.append(bfull[half * 256:(half + 1) * 256, t * 256:(t + 1) * 256])
        ct.append(cfull[t * 256:(t + 1) * 256, half * 256:(half + 1) * 256])
    bt = jnp.stack(bt).astype(BF16)
    ct = jnp.stack(ct).astype(BF16)
    rows = jnp.arange(8)[:, None]
    tabs = []
    for dstep in (1, 2, 4):
        pr, pi = power(float(dstep))
        keep = (rows >= dstep).astype(F32)
        tabs += [keep * pr[None, :], keep * pi[None, :]]
    prs, pis = zip(*[power(float(i + 1)) for i in range(8)])
    tabs += [jnp.stack(prs), jnp.stack(pis)]
    return bt, ct, jnp.concatenate(tabs, axis=1)


def _s5_kernel(u_ref, bt_ref, ct_ref, tab_ref, d_ref, gw_ref, gb_ref, o_ref, bu_scr, car_scr, *, ts):
    @pl.when(pl.program_id(1) == 0)
    def _():
        car_scr[...] = jnp.zeros_like(car_scr)

    u = u_ref[0]
    for t in range(16):
        half = (t % 8) // 4
        bu_scr[:, t * 256:(t + 1) * 256] = _dot(u[:, half * 256:(half + 1) * 256], bt_ref[t])

    def tab(idx, j):
        return tab_ref[:, idx * S5_LANES + j * LANE: idx * S5_LANES + (j + 1) * LANE]

    for j0 in range(0, _S5_COLS, _S5_CG):
        cols = list(range(j0, j0 + _S5_CG))
        re_sl = [slice(j * LANE, (j + 1) * LANE) for j in cols]
        im_sl = [slice(S5_LANES + j * LANE, S5_LANES + (j + 1) * LANE) for j in cols]
        init = tuple(car_scr[:, sl] for sl in re_sl) + tuple(car_scr[:, sl] for sl in im_sl)

        def body(kk, carry, cols=cols, re_sl=re_sl, im_sl=im_sl):
            r0 = pl.multiple_of(kk * 8, 8)
            new = [None] * (2 * len(cols))
            for n, j in enumerate(cols):
                cr, ci = carry[n], carry[len(cols) + n]
                vr = bu_scr[pl.ds(r0, 8), re_sl[n]]
                vi = bu_scr[pl.ds(r0, 8), im_sl[n]]
                for si, dstep in enumerate((1, 2, 4)):
                    lr_, li_ = tab(2 * si, j), tab(2 * si + 1, j)
                    sr = pltpu.roll(vr, dstep, 0)
                    sm = pltpu.roll(vi, dstep, 0)
                    vr, vi = vr + lr_ * sr - li_ * sm, vi + lr_ * sm + li_ * sr
                pr_, pi_ = tab(6, j), tab(7, j)
                vr, vi = vr + pr_ * cr - pi_ * ci, vi + pr_ * ci + pi_ * cr
                bu_scr[pl.ds(r0, 8), re_sl[n]] = vr
                bu_scr[pl.ds(r0, 8), im_sl[n]] = vi
                new[n] = jnp.broadcast_to(vr[7:8, :], (8, LANE))
                new[len(cols) + n] = jnp.broadcast_to(vi[7:8, :], (8, LANE))
            return tuple(new)

        fin = lax.fori_loop(0, ts // 8, body, init)
        for n in range(len(cols)):
            car_scr[:, re_sl[n]] = fin[n]
            car_scr[:, im_sl[n]] = fin[len(cols) + n]

    ys = []
    for half in range(2):
        acc = None
        for t in [half * 4 + i for i in range(4)] + [8 + half * 4 + i for i in range(4)]:
            part = _dot(bu_scr[:, t * 256:(t + 1) * 256].astype(BF16), ct_ref[t])
            acc = part if acc is None else acc + part
        ys.append(acc)
    y = jnp.concatenate(ys, axis=1) + d_ref[...] * u.astype(F32)
    g = 0.5 * y * (1.0 + jnp.tanh(math.sqrt(2.0 / math.pi) * (y + 0.044715 * (y * y * y))))
    zz = _dot(g.astype(BF16), gw_ref[...]) + gb_ref[...]
    o_ref[0] = (g * jax.nn.sigmoid(zz)).astype(o_ref.dtype)


def _s5(qkvu, bt, ct, tab, d_skip, glu_w, glu_b):
    bsz, s, _ = qkvu.shape
    ts = min(TS, s)
    const = lambda shape: pl.BlockSpec(shape, lambda b, t: (0,) * len(shape))
    return pl.pallas_call(
        functools.partial(_s5_kernel, ts=ts),
        out_shape=jax.ShapeDtypeStruct((bsz, s, S5_WIDTH), BF16),
        grid=(bsz, s // ts),
        in_specs=[
            pl.BlockSpec((1, ts, S5_WIDTH), lambda b, t: (b, t, 2)),
            const((16, 256, 256)), const((16, 256, 256)), const((8, 8 * S5_LANES)),
            const((1, S5_WIDTH)), const((S5_WIDTH, S5_WIDTH)), const((1, S5_WIDTH)),
        ],
        out_specs=pl.BlockSpec((1, ts, S5_WIDTH), lambda b, t: (b, t, 0)),
        scratch_shapes=[pltpu.VMEM((ts, 2 * S5_LANES), F32), pltpu.VMEM((8, 2 * S5_LANES), F32)],
        compiler_params=pltpu.CompilerParams(
            dimension_semantics=("arbitrary", "arbitrary"), vmem_limit_bytes=48 * 1024 * 1024),
        name="s5",
    )(qkvu, bt, ct, tab, d_skip.reshape(1, S5_WIDTH), glu_w.astype(BF16), glu_b.reshape(1, S5_WIDTH))


_ATT_STRIP = 256
_ATT_AHEAD = 3


def _alibi_tables(tq, tk):
    parts, rem = [], math.log2(math.e)
    for _ in range(3):
        p = float(np.asarray(rem, dtype=BF16))
        parts.append(p)
        rem -= p
    qa = ((np.arange(tq) // 64) * 64).astype(np.float32)
    qb = (np.arange(tq) % 64).astype(np.float32)
    ka = ((np.arange(tk) // 64) * 64).astype(np.float32)
    kb = (np.arange(tk) % 64).astype(np.float32)
    qx = np.zeros((DIFF_HEADS, tq, LANE), np.float32)
    kx = np.zeros((DIFF_HEADS, tk, LANE), np.float32)
    cval = np.zeros((DIFF_HEADS, 8, LANE), np.float32)
    for h in range(DIFF_HEADS):
        slope = 2.0 ** (-8.0 * (h + 1) / DIFF_HEADS)
        for n, p in enumerate(parts):
            qx[h, :, n] = qx[h, :, 3 + n] = slope * p
            kx[h, :, n] = ka
            kx[h, :, 3 + n] = kb
            qx[h, :, 6 + n] = -qa
            qx[h, :, 9 + n] = -qb
            kx[h, :, 6 + n] = kx[h, :, 9 + n] = slope * p
        cval[h] = slope * math.log2(math.e)
    return jnp.asarray(qx, BF16), jnp.asarray(kx, BF16), jnp.asarray(cval, F32)


def _attn_kernel(q_ref, k_ref, vt_ref, qx_ref, kx_ref, c_ref, lam_ref, sg_ref, o_ref,
                 m_scr, l_scr, acc_scr, *, tq, tk, lambda_init):
    qi = pl.program_id(2)
    c11 = c_ref[0, 0:1, 0:1]
    qe = jnp.concatenate([q_ref[0], qx_ref[0]], axis=1)
    lane = lax.broadcasted_iota(jnp.int32, (1, 2 * LANE), 1)
    zero = jnp.zeros_like(qe)
    qh = (jnp.where((lane < DIFF_DH) | (lane >= LANE), qe, zero), jnp.where(lane >= DIFF_DH, qe, zero))
    kx = kx_ref[0]
    nstrip = tq // _ATT_STRIP
    kpq = tq // tk

    m_scr[...] = jnp.full_like(m_scr, NEG_BIG)
    l_scr[...] = jnp.zeros_like(l_scr)
    acc_scr[...] = jnp.zeros_like(acc_scr)

    def block(kj, diag):
        start = pl.multiple_of(kj * tk, tk)
        ke = jnp.concatenate([k_ref[0, pl.ds(start, tk), :], kx], axis=1)
        vt = vt_ref[0, kj]
        shift = c11 * (kj * tk - qi * tq).astype(F32)
        units = []
        for half in range(2):
            for st in range(nstrip):
                nk = tk if diag is None else min(tk, (st + 1) * _ATT_STRIP - diag * tk)
                if nk > 0:
                    units.append((half, st, nk))

        def scores(unit):
            half, st, nk = unit
            return _dot_nt(ke[:nk], qh[half][st * _ATT_STRIP:(st + 1) * _ATT_STRIP])

        pending = [scores(u) for u in units[:_ATT_AHEAD]]
        for n, (half, st, nk) in enumerate(units):
            s_t = pending.pop(0)
            if n + _ATT_AHEAD < len(units):
                pending.append(scores(units[n + _ATT_AHEAD]))
            cols = slice(st * _ATT_STRIP, (st + 1) * _ATT_STRIP)
            if diag is not None:
                jrow = lax.broadcasted_iota(jnp.int32, (nk, _ATT_STRIP), 0)
                icol = lax.broadcasted_iota(jnp.int32, (nk, _ATT_STRIP), 1)
                s_t = jnp.where(jrow + diag * tk <= icol + st * _ATT_STRIP, s_t, NEG_BIG)
            m_old = m_scr[half, :, cols] - shift
            m_new = jnp.maximum(m_old, jnp.max(s_t, axis=0, keepdims=True))
            alpha = jnp.exp2(m_old - m_new)
            p = jnp.exp2(s_t - m_new)
            l_scr[half, :, cols] = alpha * l_scr[half, :, cols] + jnp.sum(p, axis=0, keepdims=True)
            acc_scr[half, :, cols] = alpha * acc_scr[half, :, cols] + _dot(vt[:, :nk], p.astype(BF16))
            m_scr[half, :, cols] = m_new + shift

    def body(kj, carry):
        block(kj, None)
        return carry

    lax.fori_loop(0, qi * kpq, body, 0)
    for d in range(kpq):
        block(qi * kpq + d, d)

    lam = (jnp.exp(jnp.sum(lam_ref[0:1, :] * lam_ref[1:2, :], axis=-1, keepdims=True))
           - jnp.exp(jnp.sum(lam_ref[2:3, :] * lam_ref[3:4, :], axis=-1, keepdims=True)) + lambda_init)
    o_t = acc_scr[0] / l_scr[0] - lam * (acc_scr[1] / l_scr[1])
    o = o_t.T
    ms = jnp.mean(o * o, axis=-1, keepdims=True)
    o = o * lax.rsqrt(ms + EPS) * sg_ref[...] * (1.0 - lambda_init)
    o_ref[0] = o.astype(o_ref.dtype)


def _diff_attention(qku, vt, lam_rows, subln_g, lambda_init):
    bsz, s, _ = qku.shape
    tk = vt.shape[3]
    tq = min(TQ, s)
    qx, kx, cval = _alibi_tables(tq, tk)
    return pl.pallas_call(
        functools.partial(_attn_kernel, tq=tq, tk=tk, lambda_init=lambda_init),
        out_shape=jax.ShapeDtypeStruct((bsz, s, DIFF_HEADS * DIFF_DV), BF16),
        grid=(bsz, DIFF_HEADS, s // tq),
        in_specs=[
            pl.BlockSpec((1, tq, LANE), lambda b, h, i: (b, i, h)),
            pl.BlockSpec((1, s, LANE), lambda b, h, i: (b, 0, DIFF_HEADS + h)),
            pl.BlockSpec((1, s // tk, DIFF_DV, tk), lambda b, h, i: (b, 0, h, 0)),
            pl.BlockSpec((1, tq, LANE), lambda b, h, i: (h, 0, 0)),
            pl.BlockSpec((1, tk, LANE), lambda b, h, i: (h, 0, 0)),
            pl.BlockSpec((1, 8, LANE), lambda b, h, i: (h, 0, 0)),
            pl.BlockSpec((8, LANE), lambda b, h, i: (0, 0)),
            pl.BlockSpec((1, LANE), lambda b, h, i: (0, 0)),
        ],
        out_specs=pl.BlockSpec((1, tq, LANE), lambda b, h, i: (b, i, h)),
        scratch_shapes=[
            pltpu.VMEM((2, 1, tq), F32), pltpu.VMEM((2, 1, tq), F32), pltpu.VMEM((2, DIFF_DV, tq), F32)],
        compiler_params=pltpu.CompilerParams(
            dimension_semantics=("arbitrary", "arbitrary", "arbitrary"),
            vmem_limit_bytes=48 * 1024 * 1024),
        name="diff_attn",
    )(qku, qku, vt, qx, kx, cval, lam_rows, subln_g.reshape(1, DIFF_DV))


def _merge_kernel(x_ref, mod_ref, g_ref, wg_ref, oa_ref, ob_ref, oc_ref, wa_ref, wb_ref, wc_ref,
                  wo_ref, o_ref):
    b = pl.program_id(0)
    x = x_ref[0]
    h = _norm_mod(x, g_ref[...], _mod_rows(mod_ref, b, 1), _mod_rows(mod_ref, b, 0)).astype(BF16)
    merged = None
    for n, (ob, wb) in enumerate(((oa_ref, wa_ref), (ob_ref, wb_ref), (oc_ref, wc_ref))):
        gate = jax.nn.sigmoid(_dot(h, wg_ref[:, n * D_MODEL:(n + 1) * D_MODEL]))
        term = gate * _dot(ob[0], wb[...])
        merged = term if merged is None else merged + term
    y = _dot(merged.astype(BF16), wo_ref[...])
    o_ref[0] = x + _mod_rows(mod_ref, b, 2) * y


def _merge(x, mod, layer, g, w_gate, o_a, o_b, o_c, w_a, w_b, w_c, w_out):
    bsz, s, d = x.shape
    tm = min(TM_MERGE, s)
    const = lambda shape: pl.BlockSpec(shape, lambda b, i: (0,) * len(shape))
    tok = lambda n: pl.BlockSpec((1, tm, n), lambda b, i: (b, i, 0))
    return pl.pallas_call(
        _merge_kernel,
        out_shape=jax.ShapeDtypeStruct((bsz, s, d), F32),
        grid=(bsz, s // tm),
        in_specs=[
            tok(d),
            pl.BlockSpec((1, 8, mod.shape[2]), lambda b, i: (layer, 0, 0)),
            const((1, d)), const((d, 3 * d)),
            tok(GLA_V), tok(S5_WIDTH), tok(DIFF_HEADS * DIFF_DV),
            const((GLA_V, d)), const((S5_WIDTH, d)), const((DIFF_HEADS * DIFF_DV, d)), const((d, d)),
        ],
        out_specs=tok(d),
        compiler_params=pltpu.CompilerParams(
            dimension_semantics=("arbitrary", "arbitrary"), vmem_limit_bytes=56 * 1024 * 1024),
        name="merge",
    )(x, mod, g.reshape(1, d), w_gate, o_a, o_b, o_c, w_a, w_b, w_c, w_out)


_HALO = 16


def _ffn_kernel(x_ref, mod_ref, g_ref, wup_ref, cw_ref, cb_ref, wd_ref,
                o_ref, h_scr, halo_scr, up_scr, act_scr, *, tm, tf, f):
    b = pl.program_id(0)

    @pl.when(pl.program_id(1) == 0)
    def _():
        halo_scr[...] = jnp.zeros_like(halo_scr)

    x = x_ref[0]
    h = _norm_mod(x, g_ref[...], _mod_rows(mod_ref, b, 4), _mod_rows(mod_ref, b, 3)).astype(BF16)
    h_scr[0:_HALO, :] = halo_scr[...]
    h_scr[_HALO:, :] = h
    halo_scr[...] = h[tm - _HALO:, :]
    hh = h_scr[...]

    def conv(part, col):
        cw = cw_ref[:, col:col + tf]
        taps = [cw[n:n + 1, :] * up_scr[part, pl.ds(_HALO - 2 + n, tm), :] for n in range(3)]
        return taps[0] + taps[1] + taps[2] + cb_ref[:, col:col + tf]

    for c in range(f // tf):
        for part in range(2):
            col = part * f + c * tf
            up_scr[part] = _dot(hh, wup_ref[:, col:col + tf])
        a = conv(0, c * tf)
        gg = conv(1, f + c * tf)
        act_scr[:, c * tf:(c + 1) * tf] = (a * jax.nn.sigmoid(a) * gg).astype(BF16)

    o_ref[0] = x + _mod_rows(mod_ref, b, 5) * _dot(act_scr[...], wd_ref[...])


def _ffn(x, mod, layer, g, w_up, conv_w, conv_b, w_down):
    bsz, s, d = x.shape
    f = w_down.shape[0]
    tm = min(TM_FFN, s)
    tf = TF
    resident = lambda shape: pl.BlockSpec(shape, lambda b, i: (0,) * len(shape), pipeline_mode=pl.Buffered(1))
    return pl.pallas_call(
        functools.partial(_ffn_kernel, tm=tm, tf=tf, f=f),
        out_shape=jax.ShapeDtypeStruct((bsz, s, d), F32),
        grid=(bsz, s // tm),
        in_specs=[
            pl.BlockSpec((1, tm, d), lambda b, i: (b, i, 0)),
            pl.BlockSpec((1, 8, mod.shape[2]), lambda b, i: (layer, 0, 0)),
            resident((1, d)), resident((d, 2 * f)), resident((3, 2 * f)), resident((1, 2 * f)),
            resident((f, d)),
        ],
        out_specs=pl.BlockSpec((1, tm, d), lambda b, i: (b, i, 0)),
        scratch_shapes=[
            pltpu.VMEM((tm + _HALO, d), BF16), pltpu.VMEM((_HALO, d), BF16),
            pltpu.VMEM((2, tm + _HALO, tf), F32), pltpu.VMEM((tm, f), BF16)],
        compiler_params=pltpu.CompilerParams(
            dimension_semantics=("arbitrary", "arbitrary"), vmem_limit_bytes=56 * 1024 * 1024),
        name="ffn",
    )(x, mod, g.reshape(1, d), w_up, conv_w, conv_b.reshape(1, 2 * f), w_down)


def kernel(x, c, ada_w, ada_b, norm1_g, w_in, gla_gk_w, gla_gk_b, gla_norm_g, s5_lambda_re, s5_lambda_im, s5_b_re, s5_b_im, s5_c_re, s5_c_im, s5_d, s5_log_dt, s5_glu_w, s5_glu_b, diff_q_norm_g, diff_k_norm_g, diff_lambda_q1, diff_lambda_k1, diff_lambda_q2, diff_lambda_k2, diff_subln_g, w_branch_gla, w_branch_s5, w_branch_diff, w_out, norm2_g, ffn_w_up, ffn_conv_w, ffn_conv_b, ffn_w_down):
    depth = ada_w.shape[0]
    mod = _modulation(c, ada_w, ada_b)
    for l in range(depth):
        lambda_init = 0.8 - 0.6 * math.exp(-0.3 * l)
        wl = w_in[l]
        w_gla = jnp.concatenate(
            [wl[:, _OFF_GLA:_OFF_GLR], wl[:, _OFF_GLR:_OFF_SU],
             jnp.zeros((D_MODEL, LANE - GLA_LR), F32)], axis=1).astype(BF16)
        off_dv = _OFF_DQ + 2 * DIFF_HEADS * 2 * DIFF_DH
        w_qku = jnp.concatenate([wl[:, _OFF_DQ:off_dv], wl[:, _OFF_SU:_OFF_DQ]], axis=1).astype(BF16)
        w_vt = wl[:, off_dv:_OFF_GATES].T.astype(BF16)
        w_gate = wl[:, _OFF_GATES:].astype(BF16)

        p_gla = _inproj_gla(x, mod, l, norm1_g[l], w_gla)
        qku, vt = _inproj_qkuv(x, mod, l, norm1_g[l], w_qku, w_vt, diff_q_norm_g[l], diff_k_norm_g[l])

        o_a = _gla(p_gla, gla_gk_w[l], gla_gk_b[l], gla_norm_g[l])
        bt, ct, tab = _s5_tables(s5_lambda_re[l], s5_lambda_im[l], s5_b_re[l], s5_b_im[l],
                                 s5_c_re[l], s5_c_im[l], s5_log_dt[l])
        o_b = _s5(qku, bt, ct, tab, s5_d[l], s5_glu_w[l], s5_glu_b[l])
        lam_rows = jnp.zeros((8, LANE), F32).at[0:4, 0:DIFF_DH].set(
            jnp.stack([diff_lambda_q1[l], diff_lambda_k1[l], diff_lambda_q2[l], diff_lambda_k2[l]]))
        o_c = _diff_attention(qku, vt, lam_rows, diff_subln_g[l], lambda_init)

        x = _merge(x, mod, l, norm1_g[l], w_gate, o_a, o_b, o_c,
                   w_branch_gla[l].astype(BF16), w_branch_s5[l].astype(BF16),
                   w_branch_diff[l].astype(BF16), w_out[l].astype(BF16))
        x = _ffn(x, mod, l, norm2_g[l], ffn_w_up[l].astype(BF16), ffn_conv_w[l], ffn_conv_b[l],
                 ffn_w_down[l].astype(BF16))
    return x
```

```python
import functools
import math

import jax
import jax.numpy as jnp
import numpy as np
from jax import lax
from jax.experimental import pallas as pl
from jax.experimental.pallas import tpu as pltpu

F32 = jnp.float32
BF16 = jnp.bfloat16

D_MODEL = 1024
GLA_HEADS = 4
GLA_DK = 64
GLA_DV = 128
GLA_LR = 16
GLA_TAU = 16.0
GLA_CHUNK = 64
GLA_QK = GLA_HEADS * GLA_DK
GLA_V = GLA_HEADS * GLA_DV
GLA_COLS = 2 * GLA_QK + 2 * GLA_V + 128
S5_WIDTH = 512
S5_GROUP = 16
S5_GROUPS = 32
S5_STATE = 64
S5_LANES = S5_GROUPS * S5_STATE
DIFF_HEADS = 4
DIFF_DH = 64
DIFF_DV = 128
D_FF = 2816
EPS = 1e-6
LANE = 128
NEG_BIG = -1e30

TM_IN = 1024
TG = 256
TS = 256
TQ = 1024
TK = 512
TM_MERGE = 512
TM_FFN = 512
TF = 256

_OFF_GLA = 0
_OFF_GLR = 1536
_OFF_SU = 1552
_OFF_DQ = 2064
_OFF_GATES = 3600
_IN_COLS = 6672

_W_GATES = 3 * D_MODEL
_W_GLA = 2 * GLA_QK + 2 * GLA_V
_W_GLA_BLK = _W_GATES // _W_GLA
_W_TILE = 512
_W_U_BLK = (_W_GATES + _W_GLA) // _W_TILE


def _dot(a, b):
    return jnp.dot(a, b, preferred_element_type=F32)


def _dot_nt(a, b):
    return lax.dot_general(a, b, (((1,), (1,)), ((), ())), preferred_element_type=F32)


def _dot_tn(a, b):
    return lax.dot_general(a, b, (((0,), (0,)), ((), ())), preferred_element_type=F32)


def _norm_mod(x, g, sc, sh):
    ms = jnp.mean(x * x, axis=-1, keepdims=True)
    y = x * lax.rsqrt(ms + EPS) * g
    return y * (1.0 + sc) + sh


def _mod_rows(mod_ref, b, k):
    return mod_ref[0, pl.ds(b, 1), k * D_MODEL:(k + 1) * D_MODEL]


def _mod_kernel(c_ref, w_ref, b_ref, o_ref):
    c = c_ref[...]
    ca = (c * jax.nn.sigmoid(c)).astype(BF16)
    o_ref[0] = _dot(ca, w_ref[0].astype(BF16)) + b_ref[0]


def _modulation(c, ada_w, ada_b):
    depth, d, n = ada_w.shape
    bsz = c.shape[0]
    cp = jnp.zeros((8, d), F32).at[:bsz].set(c)
    tn = 1536
    return pl.pallas_call(
        _mod_kernel,
        out_shape=jax.ShapeDtypeStruct((depth, 8, n), F32),
        grid=(depth, n // tn),
        in_specs=[
            pl.BlockSpec((8, d), lambda l, j: (0, 0)),
            pl.BlockSpec((1, d, tn), lambda l, j: (l, 0, j)),
            pl.BlockSpec((1, 1, tn), lambda l, j: (l, 0, j)),
        ],
        out_specs=pl.BlockSpec((1, 8, tn), lambda l, j: (l, 0, j)),
        name="adaln_mod",
    )(cp, ada_w, ada_b.reshape(depth, 1, n))


def _seg_rmsnorm(a, g128, scale):
    lane = lax.broadcasted_iota(jnp.int32, (1, LANE), 1)
    lo = (lane < DIFF_DH).astype(F32)
    hi = 1.0 - lo
    outs = []
    for hb in range(a.shape[-1] // LANE):
        xh = a[:, hb * LANE:(hb + 1) * LANE]
        sq = xh * xh
        s_lo = jnp.sum(sq * lo, axis=-1, keepdims=True)
        s_hi = jnp.sum(sq * hi, axis=-1, keepdims=True)
        ms = (s_lo * lo + s_hi * hi) * (1.0 / DIFF_DH)
        outs.append(xh * lax.rsqrt(ms + EPS) * g128 * scale)
    return jnp.concatenate(outs, axis=-1)


def _inproj_gla_kernel(x_ref, mod_ref, g_ref, w_ref, wlr_ref, o_ref):
    b = pl.program_id(0)
    h = _norm_mod(x_ref[0], g_ref[0], _mod_rows(mod_ref, b, 1), _mod_rows(mod_ref, b, 0)).astype(BF16)
    o_ref[0, :, 0:_W_GLA] = _dot(h, w_ref[0])
    o_ref[0, :, _W_GLA:] = _dot(h, wlr_ref[0])


def _inproj_gla(x, mod, layer, g, w_all, w_lr):
    bsz, s, d = x.shape
    tm = min(TM_IN, s)
    return pl.pallas_call(
        _inproj_gla_kernel,
        out_shape=jax.ShapeDtypeStruct((bsz, s, GLA_COLS), F32),
        grid=(bsz, s // tm),
        in_specs=[
            pl.BlockSpec((1, tm, d), lambda b, i: (b, i, 0)),
            pl.BlockSpec((1, 8, mod.shape[2]), lambda b, i: (layer, 0, 0)),
            pl.BlockSpec((1, 1, d), lambda b, i: (layer, 0, 0)),
            pl.BlockSpec((1, d, _W_GLA), lambda b, i: (layer, 0, _W_GLA_BLK)),
            pl.BlockSpec((1, d, LANE), lambda b, i: (layer, 0, 0)),
        ],
        out_specs=pl.BlockSpec((1, tm, GLA_COLS), lambda b, i: (b, i, 0)),
        compiler_params=pltpu.CompilerParams(
            dimension_semantics=("arbitrary", "arbitrary"), vmem_limit_bytes=48 * 1024 * 1024),
        name="inproj_gla",
    )(x, mod, g, w_all, w_lr)


def _inproj_qkuv_kernel(x_ref, mod_ref, g_ref, w_ref, wvt_ref, qg_ref, kg_ref, o_ref, vt_ref, h_scr):
    b = pl.program_id(0)
    j = pl.program_id(2)

    @pl.when(j == 0)
    def _():
        h = _norm_mod(x_ref[0], g_ref[0], _mod_rows(mod_ref, b, 1), _mod_rows(mod_ref, b, 0))
        h_scr[...] = h.astype(BF16)

    @pl.when(j == 0)
    def _():
        acc = _dot(h_scr[...], w_ref[0])
        o_ref[0] = _seg_rmsnorm(acc, qg_ref[0], DIFF_DH ** -0.5 * math.log2(math.e)).astype(o_ref.dtype)

    @pl.when(j == 1)
    def _():
        acc = _dot(h_scr[...], w_ref[0])
        o_ref[0] = _seg_rmsnorm(acc, kg_ref[0], 1.0).astype(o_ref.dtype)

    @pl.when(j == 2)
    def _():
        o_ref[0] = _dot(h_scr[...], w_ref[0]).astype(o_ref.dtype)

    @pl.when(j == 3)
    def _():
        vt = _dot_nt(wvt_ref[0], h_scr[...]).astype(vt_ref.dtype)
        tk = vt_ref.shape[3]
        for n in range(vt_ref.shape[1]):
            vt_ref[0, n] = vt[:, n * tk:(n + 1) * tk]


def _inproj_qkuv(x, mod, layer, g, w_all, w_vt, q_gain, k_gain):
    bsz, s, d = x.shape
    tm = min(TM_IN, s)
    tq = min(TK, s)
    tn = _W_TILE
    nv = w_vt.shape[1]
    wblk = lambda j: _W_U_BLK + jnp.where(j == 0, 1, jnp.where(j == 1, 2, 0))
    return pl.pallas_call(
        _inproj_qkuv_kernel,
        out_shape=(jax.ShapeDtypeStruct((bsz, s, 3 * tn), BF16),
                   jax.ShapeDtypeStruct((bsz, s // tq, nv, tq), BF16)),
        grid=(bsz, s // tm, 4),
        in_specs=[
            pl.BlockSpec((1, tm, d), lambda b, i, j: (b, i, 0)),
            pl.BlockSpec((1, 8, mod.shape[2]), lambda b, i, j: (layer, 0, 0)),
            pl.BlockSpec((1, 1, d), lambda b, i, j: (layer, 0, 0)),
            pl.BlockSpec((1, d, tn), lambda b, i, j: (layer, 0, wblk(j))),
            pl.BlockSpec((1, nv, d), lambda b, i, j: (layer, 0, 0)),
            pl.BlockSpec((1, 1, LANE), lambda b, i, j: (layer, 0, 0)),
            pl.BlockSpec((1, 1, LANE), lambda b, i, j: (layer, 0, 0)),
        ],
        out_specs=(
            pl.BlockSpec((1, tm, tn), lambda b, i, j: (b, i, jnp.minimum(j, 2))),
            pl.BlockSpec((1, tm // tq, nv, tq), lambda b, i, j: (b, i, 0, 0)),
        ),
        scratch_shapes=[pltpu.VMEM((tm, d), BF16)],
        compiler_params=pltpu.CompilerParams(
            dimension_semantics=("arbitrary", "arbitrary", "arbitrary"),
            vmem_limit_bytes=48 * 1024 * 1024),
        name="inproj_qkuv",
    )(x, mod, g, w_all, w_vt, q_gain, k_gain)


def _gla_kernel(p_ref, gkw_ref, gkb_ref, ng_ref, ltri_ref, lall_ref, o_ref, s_scr, *, nchunk):
    @pl.when(pl.program_id(1) == 0)
    def _():
        s_scr[...] = jnp.zeros_like(s_scr)

    blk = p_ref[0]
    q = blk[:, 0:GLA_QK]
    k = blk[:, GLA_QK:2 * GLA_QK]
    v = blk[:, 2 * GLA_QK:2 * GLA_QK + GLA_V]
    r = blk[:, 2 * GLA_QK + GLA_V:2 * GLA_QK + 2 * GLA_V]
    lr = blk[:, 2 * GLA_QK + 2 * GLA_V:]
    z = _dot(lr.astype(BF16), gkw_ref[...]) + gkb_ref[...]
    la = (jnp.minimum(z, 0.0) - jnp.log1p(jnp.exp(-jnp.abs(z)))) * (1.0 / GLA_TAU)
    la_hi = la.astype(BF16)
    la_lo = (la - la_hi.astype(F32)).astype(BF16)
    ltri = ltri_ref[...]
    lall = lall_ref[...]
    bcum = _dot(ltri, la_hi) + _dot(ltri, la_lo)
    btot = _dot(lall, la_hi) + _dot(lall, la_lo)
    qd = q * (GLA_DK ** -0.5) * jnp.exp(bcum)
    kd = (k * jnp.exp(-bcum)).astype(BF16)
    ke = (k * jnp.exp(btot - bcum)).astype(BF16)
    vb = v.astype(BF16)
    gate = r * jax.nn.sigmoid(r)

    lane_head = lax.broadcasted_iota(jnp.int32, (1, GLA_QK), 1) // GLA_DK
    rowi = lax.broadcasted_iota(jnp.int32, (GLA_QK, GLA_CHUNK), 0) % GLA_CHUNK
    colj = lax.broadcasted_iota(jnp.int32, (GLA_QK, GLA_CHUNK), 1)
    tril = rowi >= colj
    ones_cv = jnp.ones((GLA_CHUNK, GLA_DV), BF16)
    ng = ng_ref[...]

    for c in range(nchunk):
        rows = slice(c * GLA_CHUNK, (c + 1) * GLA_CHUNK)
        qdc = qd[rows]
        qm = jnp.concatenate(
            [jnp.where(lane_head == h, qdc, 0.0).astype(BF16) for h in range(GLA_HEADS)], axis=0)
        att = jnp.where(tril, _dot_nt(qm, kd[rows]), 0.0).astype(BF16)
        state = s_scr[...]
        o_inter = _dot(qm, state.astype(BF16))
        vc = vb[rows]
        outs = []
        for h in range(GLA_HEADS):
            hr = slice(h * GLA_CHUNK, (h + 1) * GLA_CHUNK)
            o_h = _dot(att[hr], vc[:, h * GLA_DV:(h + 1) * GLA_DV]) + o_inter[hr]
            ms = jnp.mean(o_h * o_h, axis=-1, keepdims=True)
            outs.append(o_h * lax.rsqrt(ms + EPS) * ng)
        o = jnp.concatenate(outs, axis=1) * gate[rows]
        o_ref[0, rows, :] = o.astype(o_ref.dtype)

        kv = _dot_tn(ke[rows], vc)
        kvd = jnp.concatenate(
            [kv[h * GLA_DK:(h + 1) * GLA_DK, h * GLA_DV:(h + 1) * GLA_DV] for h in range(GLA_HEADS)],
            axis=0)
        ldec = _dot_tn(la_hi[rows], ones_cv) + _dot_tn(la_lo[rows], ones_cv)
        s_scr[...] = jnp.exp(ldec) * state + kvd


def _gla(p, gk_w, gk_b, norm_g):
    bsz, s, _ = p.shape
    tg = min(TG, s)
    nchunk = tg // GLA_CHUNK
    gkw = jnp.zeros((LANE, GLA_QK), F32).at[:GLA_LR].set(gk_w).astype(BF16)
    ri = jnp.arange(tg)[:, None]
    ci = jnp.arange(tg)[None, :]
    same = (ri // GLA_CHUNK) == (ci // GLA_CHUNK)
    ltri = (same & (ri >= ci)).astype(BF16)
    lall = same.astype(BF16)
    const = lambda shape: pl.BlockSpec(shape, lambda b, t: (0,) * len(shape))
    return pl.pallas_call(
        functools.partial(_gla_kernel, nchunk=nchunk),
        out_shape=jax.ShapeDtypeStruct((bsz, s, GLA_V), BF16),
        grid=(bsz, s // tg),
        in_specs=[
            pl.BlockSpec((1, tg, GLA_COLS), lambda b, t: (b, t, 0)),
            const((LANE, GLA_QK)), const((1, GLA_QK)), const((1, GLA_DV)),
            const((tg, tg)), const((tg, tg)),
        ],
        out_specs=pl.BlockSpec((1, tg, GLA_V), lambda b, t: (b, t, 0)),
        scratch_shapes=[pltpu.VMEM((GLA_QK, GLA_DV), F32)],
        compiler_params=pltpu.CompilerParams(dimension_semantics=("arbitrary", "arbitrary")),
        name="gla",
    )(p, gkw, gk_b.reshape(1, GLA_QK), norm_g.reshape(1, GLA_DV), ltri, lall)


_S5_COLS = S5_LANES // LANE
_S5_CG = 8
_S5_SEG = 8


def _s5_tables(lam_re, lam_im, b_re, b_im, c_re, c_im, log_dt, sl):
    nl = lam_re.shape[0]
    dt = jnp.exp(log_dt.astype(F32))[..., None]
    ar = (lam_re.astype(F32) * dt).reshape(nl, 1, S5_LANES)
    ai = (lam_im.astype(F32) * dt).reshape(nl, 1, S5_LANES)

    def powers(ks):
        kk = jnp.asarray(ks, F32)[None, :, None]
        mag = jnp.exp(kk * ar)
        return mag * jnp.cos(kk * ai), mag * jnp.sin(kk * ai)

    lr1, li1 = powers([1.0])
    lbr, lbi = lr1.reshape(lam_re.shape), li1.reshape(lam_re.shape)
    den = lam_re * lam_re + lam_im * lam_im
    nr = ((lbr - 1.0) * lam_re + lbi * lam_im) / den
    ni = (lbi * lam_re - (lbr - 1.0) * lam_im) / den
    bbr = nr[..., None] * b_re - ni[..., None] * b_im
    bbi = nr[..., None] * b_im + ni[..., None] * b_re
    eye4 = jnp.eye(4, dtype=F32)
    band = eye4[np.arange(8) % 4]

    def b_tiles(bb):
        bb4 = bb.reshape(nl, 8, 4, S5_STATE, S5_GROUP)
        small = jnp.einsum('lnkph,kj->lnkhjp', bb4, eye4).reshape(nl, 8, 64, 256)
        return jnp.einsum('lnrc,nq->lnqrc', small, band).reshape(nl, 8, 256, 256)

    def c_tiles(cc):
        cc4 = cc.reshape(nl, 8, 4, S5_GROUP, S5_STATE)
        small = jnp.einsum('lnkhp,kj->lnkpjh', cc4, eye4).reshape(nl, 8, 256, 64)
        return jnp.einsum('lnrc,nq->lnrqc', small, band).reshape(nl, 8, 256, 256)

    bt = jnp.concatenate([b_tiles(bbr), b_tiles(bbi)], axis=1).astype(BF16)
    ct = jnp.concatenate([c_tiles(c_re), c_tiles(-c_im)], axis=1).astype(BF16)
    lam8 = jnp.concatenate([jnp.broadcast_to(lr1, (nl, _S5_SEG, S5_LANES)),
                            jnp.broadcast_to(li1, (nl, _S5_SEG, S5_LANES))], axis=2)
    dsteps = np.array([1, 2, 4])
    keep = jnp.asarray(np.arange(_S5_SEG)[None, :] >= dsteps[:, None], F32)
    pr3, pi3 = powers(dsteps * float(sl))
    logstep = jnp.stack([keep[None, :, :, None] * pr3[:, :, None, :],
                         keep[None, :, :, None] * pi3[:, :, None, :]], axis=2)
    logstep = logstep.transpose(0, 3, 1, 2, 4).reshape(nl, _S5_SEG, 6 * S5_LANES)
    qr, qi = powers(np.arange(1, _S5_SEG + 1) * float(sl))
    seg_tab = jnp.concatenate([logstep, qr, qi], axis=2)
    pwr, pwi = powers(np.arange(1, sl + 1))
    pw = jnp.concatenate([pwr, pwi], axis=2)
    return bt, ct, lam8, seg_tab, pw


def _s5_kernel(u_ref, bt_ref, ct_ref, lam_ref, seg_ref, pw_ref, d_ref, gw_ref, gb_ref, o_ref,
               bu_scr, c_scr, x_scr, car_scr, *, ts):
    sl = ts // _S5_SEG
    pitch = sl + 8

    @pl.when(pl.program_id(1) == 0)
    def _():
        car_scr[...] = jnp.zeros_like(car_scr)

    u = u_ref[0]
    for t in range(16):
        half = (t % 8) // 4
        res = _dot(u[:, half * 256:(half + 1) * 256], bt_ref[0, t])
        for s in range(_S5_SEG):
            for w in range(2):
                bu_scr[2 * t + w, s * pitch:s * pitch + sl, :] = res[s * sl:(s + 1) * sl, w * LANE:(w + 1) * LANE]

    def lanes(ref, idx, j):
        return ref[0, :, idx * S5_LANES + j * LANE: idx * S5_LANES + (j + 1) * LANE]

    for j0 in range(0, _S5_COLS, _S5_CG):
        cols = list(range(j0, j0 + _S5_CG))

        def body(t, carry, cols=cols):
            new = [None] * (2 * len(cols))
            for n, j in enumerate(cols):
                xr, xi = carry[n], carry[len(cols) + n]
                lr_, li_ = lanes(lam_ref, 0, j), lanes(lam_ref, 1, j)
                vr = bu_scr[j, pl.ds(t, _S5_SEG, stride=pitch), :]
                vi = bu_scr[_S5_COLS + j, pl.ds(t, _S5_SEG, stride=pitch), :]
                nr = lr_ * xr - li_ * xi + vr
                ni = lr_ * xi + li_ * xr + vi
                bu_scr[j, pl.ds(t, _S5_SEG, stride=pitch), :] = nr
                bu_scr[_S5_COLS + j, pl.ds(t, _S5_SEG, stride=pitch), :] = ni
                new[n], new[len(cols) + n] = nr, ni
            return tuple(new)

        zero = jnp.zeros((_S5_SEG, LANE), F32)
        fin = lax.fori_loop(0, sl, body, (zero,) * (2 * len(cols)), unroll=4)

        row = lax.broadcasted_iota(jnp.int32, (_S5_SEG, LANE), 0)
        for n, j in enumerate(cols):
            gr, gi = fin[n], fin[len(cols) + n]
            for si, dstep in enumerate((1, 2, 4)):
                tr, ti = lanes(seg_ref, 2 * si, j), lanes(seg_ref, 2 * si + 1, j)
                sr = pltpu.roll(gr, dstep, 0)
                sm = pltpu.roll(gi, dstep, 0)
                gr, gi = gr + tr * sr - ti * sm, gi + tr * sm + ti * sr
            cr = car_scr[:, j * LANE:(j + 1) * LANE]
            ci = car_scr[:, S5_LANES + j * LANE:S5_LANES + (j + 1) * LANE]
            qr, qi = lanes(seg_ref, 6, j), lanes(seg_ref, 7, j)
            gr, gi = gr + qr * cr - qi * ci, gi + qr * ci + qi * cr
            c_scr[j] = jnp.where(row == 0, cr, pltpu.roll(gr, 1, 0))
            c_scr[_S5_COLS + j] = jnp.where(row == 0, ci, pltpu.roll(gi, 1, 0))
            car_scr[:, j * LANE:(j + 1) * LANE] = jnp.broadcast_to(gr[_S5_SEG - 1:, :], (_S5_SEG, LANE))
            car_scr[:, S5_LANES + j * LANE:S5_LANES + (j + 1) * LANE] = jnp.broadcast_to(
                gi[_S5_SEG - 1:, :], (_S5_SEG, LANE))

    for j in range(_S5_COLS):
        pr_ = pw_ref[0, :, j * LANE:(j + 1) * LANE]
        pi_ = pw_ref[0, :, S5_LANES + j * LANE:S5_LANES + (j + 1) * LANE]
        for s in range(_S5_SEG):
            cr = c_scr[j, s:s + 1, :]
            ci = c_scr[_S5_COLS + j, s:s + 1, :]
            xr = bu_scr[j, s * pitch:s * pitch + sl, :] + pr_ * cr - pi_ * ci
            xi = bu_scr[_S5_COLS + j, s * pitch:s * pitch + sl, :] + pr_ * ci + pi_ * cr
            x_scr[s * sl:(s + 1) * sl, j * LANE:(j + 1) * LANE] = xr.astype(BF16)
            x_scr[s * sl:(s + 1) * sl, S5_LANES + j * LANE:S5_LANES + (j + 1) * LANE] = xi.astype(BF16)

    ys = []
    for half in range(2):
        acc = None
        for t in [half * 4 + i for i in range(4)] + [8 + half * 4 + i for i in range(4)]:
            part = _dot(x_scr[:, t * 256:(t + 1) * 256], ct_ref[0, t])
            acc = part if acc is None else acc + part
        ys.append(acc)
    y = jnp.concatenate(ys, axis=1) + d_ref[0] * u.astype(F32)
    g = 0.5 * y * (1.0 + jnp.tanh(math.sqrt(2.0 / math.pi) * (y + 0.044715 * (y * y * y))))
    zz = _dot(g.astype(BF16), gw_ref[0]) + gb_ref[0]
    o_ref[0] = (g * jax.nn.sigmoid(zz)).astype(o_ref.dtype)


def _s5(qku, layer, tables, d_skip, glu_w, glu_b):
    bsz, s, _ = qku.shape
    ts = min(TS, s)
    sl = ts // _S5_SEG
    bt, ct, lam8, seg_tab, pw = tables
    const = lambda shape: pl.BlockSpec((1,) + shape, lambda b, t: (layer,) + (0,) * len(shape))
    return pl.pallas_call(
        functools.partial(_s5_kernel, ts=ts),
        out_shape=jax.ShapeDtypeStruct((bsz, s, S5_WIDTH), BF16),
        grid=(bsz, s // ts),
        in_specs=[
            pl.BlockSpec((1, ts, S5_WIDTH), lambda b, t: (b, t, 2)),
            const((16, 256, 256)), const((16, 256, 256)),
            const((_S5_SEG, 2 * S5_LANES)), const((_S5_SEG, 8 * S5_LANES)), const((sl, 2 * S5_LANES)),
            const((1, S5_WIDTH)), const((S5_WIDTH, S5_WIDTH)), const((1, S5_WIDTH)),
        ],
        out_specs=pl.BlockSpec((1, ts, S5_WIDTH), lambda b, t: (b, t, 0)),
        scratch_shapes=[
            pltpu.VMEM((2 * _S5_COLS, _S5_SEG * (sl + 8), LANE), F32),
            pltpu.VMEM((2 * _S5_COLS, _S5_SEG, LANE), F32),
            pltpu.VMEM((ts, 2 * S5_LANES), BF16),
            pltpu.VMEM((_S5_SEG, 2 * S5_LANES), F32)],
        compiler_params=pltpu.CompilerParams(
            dimension_semantics=("arbitrary", "arbitrary"), vmem_limit_bytes=48 * 1024 * 1024),
        name="s5",
    )(qku, bt, ct, lam8, seg_tab, pw, d_skip, glu_w, glu_b)


_ATT_STRIP = 256
_ATT_AHEAD = 3
_ATT_ONES = 16


def _alibi_tables(tq, tk):
    parts, rem = [], math.log2(math.e)
    for _ in range(3):
        p = float(np.asarray(rem, dtype=BF16))
        parts.append(p)
        rem -= p
    qa = ((np.arange(tq) // 64) * 64).astype(np.float32)
    qb = (np.arange(tq) % 64).astype(np.float32)
    ka = ((np.arange(tk) // 64) * 64).astype(np.float32)
    kb = (np.arange(tk) % 64).astype(np.float32)
    qx = np.zeros((DIFF_HEADS, tq, LANE), np.float32)
    kx = np.zeros((DIFF_HEADS, tk, LANE), np.float32)
    cval = np.zeros((DIFF_HEADS, 8, LANE), np.float32)
    for h in range(DIFF_HEADS):
        slope = 2.0 ** (-8.0 * (h + 1) / DIFF_HEADS)
        for n, p in enumerate(parts):
            qx[h, :, n] = qx[h, :, 3 + n] = slope * p
            kx[h, :, n] = ka
            kx[h, :, 3 + n] = kb
            qx[h, :, 6 + n] = -qa
            qx[h, :, 9 + n] = -qb
            kx[h, :, 6 + n] = kx[h, :, 9 + n] = slope * p
        cval[h] = slope * math.log2(math.e)
    return jnp.asarray(qx, BF16), jnp.asarray(kx, BF16), jnp.asarray(cval, F32)


def _attn_kernel(q_ref, k_ref, vt_ref, qx_ref, kx_ref, c_ref, lam_ref, sg_ref, o_ref,
                 qt_scr, m_scr, acc_scr, *, tq, tk, lambda_init):
    qi = pl.program_id(2)
    c11 = c_ref[0, 0:1, 0:1]
    qe = jnp.concatenate([q_ref[0], qx_ref[0]], axis=1).astype(F32)
    lane = lax.broadcasted_iota(jnp.int32, (1, 2 * LANE), 1)
    qt_scr[0] = jnp.where((lane < DIFF_DH) | (lane >= LANE), qe, 0.0).T.astype(BF16)
    qt_scr[1] = jnp.where(lane >= DIFF_DH, qe, 0.0).T.astype(BF16)
    kx = kx_ref[0]
    nstrip = tq // _ATT_STRIP
    kpq = tq // tk
    ones_rows = jnp.ones((_ATT_ONES, tk), BF16)

    m_scr[...] = jnp.full_like(m_scr, NEG_BIG)
    acc_scr[...] = jnp.zeros_like(acc_scr)

    def blocks(entries):
        loaded = []
        for kj, diag in entries:
            start = pl.multiple_of(kj * tk, tk)
            ke = jnp.concatenate([k_ref[0, pl.ds(start, tk), :], kx], axis=1)
            vt = jnp.concatenate([vt_ref[0, kj], ones_rows], axis=0)
            shift = c11 * (kj * tk - qi * tq).astype(F32)
            loaded.append((ke, vt, shift))
        units = []
        for e, (kj, diag) in enumerate(entries):
            for half in range(2):
                for st in range(nstrip):
                    nk = tk if diag is None else min(tk, (st + 1) * _ATT_STRIP - diag * tk)
                    if nk > 0:
                        units.append((e, half, st, nk))

        def scores(unit):
            e, half, st, nk = unit
            return _dot(loaded[e][0][:nk], qt_scr[half, :, st * _ATT_STRIP:(st + 1) * _ATT_STRIP])

        pending = [scores(u) for u in units[:_ATT_AHEAD]]
        for n, (e, half, st, nk) in enumerate(units):
            s_t = pending.pop(0)
            if n + _ATT_AHEAD < len(units):
                pending.append(scores(units[n + _ATT_AHEAD]))
            _, vt, shift = loaded[e]
            diag = entries[e][1]
            cols = slice(st * _ATT_STRIP, (st + 1) * _ATT_STRIP)
            if diag is not None and diag * tk + nk - 1 > st * _ATT_STRIP:
                jrow = lax.broadcasted_iota(jnp.int32, (nk, _ATT_STRIP), 0)
                icol = lax.broadcasted_iota(jnp.int32, (nk, _ATT_STRIP), 1)
                s_t = jnp.where(jrow + diag * tk <= icol + st * _ATT_STRIP, s_t, NEG_BIG)
            m_old = m_scr[half, :, cols] - shift
            m_new = jnp.maximum(m_old, jnp.max(s_t, axis=0, keepdims=True))
            alpha = jnp.exp2(m_old - m_new)
            p = jnp.exp2(s_t - m_new)
            acc_scr[half, :, cols] = alpha * acc_scr[half, :, cols] + _dot(vt[:, :nk], p.astype(BF16))
            m_scr[half, :, cols] = m_new + shift

    def body(i, carry):
        blocks([(i * kpq + d, None) for d in range(kpq)])
        return carry

    lax.fori_loop(0, qi, body, 0)
    blocks([(qi * kpq + d, d) for d in range(kpq)])

    lam = (jnp.exp(jnp.sum(lam_ref[0, 0:1, :] * lam_ref[0, 1:2, :], axis=-1, keepdims=True))
           - jnp.exp(jnp.sum(lam_ref[0, 2:3, :] * lam_ref[0, 3:4, :], axis=-1, keepdims=True)) + lambda_init)
    norm = [acc_scr[half, 0:DIFF_DV, :] / acc_scr[half, DIFF_DV:DIFF_DV + 1, :] for half in range(2)]
    o_t = norm[0] - lam * norm[1]
    o = o_t.T
    ms = jnp.mean(o * o, axis=-1, keepdims=True)
    o = o * lax.rsqrt(ms + EPS) * sg_ref[0] * (1.0 - lambda_init)
    o_ref[0] = o.astype(o_ref.dtype)


def _diff_attention(qku, vt, layer, lam_rows, subln_g, lambda_init):
    bsz, s, _ = qku.shape
    tk = vt.shape[3]
    tq = min(TQ, s)
    qx, kx, cval = _alibi_tables(tq, tk)
    return pl.pallas_call(
        functools.partial(_attn_kernel, tq=tq, tk=tk, lambda_init=lambda_init),
        out_shape=jax.ShapeDtypeStruct((bsz, s, DIFF_HEADS * DIFF_DV), BF16),
        grid=(bsz, DIFF_HEADS, s // tq),
        in_specs=[
            pl.BlockSpec((1, tq, LANE), lambda b, h, i: (b, i, h)),
            pl.BlockSpec((1, s, LANE), lambda b, h, i: (b, 0, DIFF_HEADS + h)),
            pl.BlockSpec((1, s // tk, DIFF_DV, tk), lambda b, h, i: (b, 0, h, 0)),
            pl.BlockSpec((1, tq, LANE), lambda b, h, i: (h, 0, 0)),
            pl.BlockSpec((1, tk, LANE), lambda b, h, i: (h, 0, 0)),
            pl.BlockSpec((1, 8, LANE), lambda b, h, i: (h, 0, 0)),
            pl.BlockSpec((1, 8, LANE), lambda b, h, i: (layer, 0, 0)),
            pl.BlockSpec((1, 1, LANE), lambda b, h, i: (layer, 0, 0)),
        ],
        out_specs=pl.BlockSpec((1, tq, LANE), lambda b, h, i: (b, i, h)),
        scratch_shapes=[
            pltpu.VMEM((2, 2 * LANE, tq), BF16), pltpu.VMEM((2, 1, tq), F32),
            pltpu.VMEM((2, DIFF_DV + _ATT_ONES, tq), F32)],
        compiler_params=pltpu.CompilerParams(
            dimension_semantics=("arbitrary", "arbitrary", "arbitrary"),
            vmem_limit_bytes=48 * 1024 * 1024),
        name="diff_attn",
    )(qku, qku, vt, qx, kx, cval, lam_rows, subln_g)


def _merge_kernel(x_ref, mod_ref, g_ref, wg_ref, oa_ref, ob_ref, oc_ref, wa_ref, wb_ref, wc_ref,
                  wo_ref, o_ref):
    b = pl.program_id(0)
    x = x_ref[0]
    h = _norm_mod(x, g_ref[0], _mod_rows(mod_ref, b, 1), _mod_rows(mod_ref, b, 0)).astype(BF16)
    merged = None
    for n, (ob, wb) in enumerate(((oa_ref, wa_ref), (ob_ref, wb_ref), (oc_ref, wc_ref))):
        gate = jax.nn.sigmoid(_dot(h, wg_ref[0, :, n * D_MODEL:(n + 1) * D_MODEL]))
        term = gate * _dot(ob[0], wb[0])
        merged = term if merged is None else merged + term
    y = _dot(merged.astype(BF16), wo_ref[0])
    o_ref[0] = x + _mod_rows(mod_ref, b, 2) * y


def _merge(x, mod, layer, g, w_all, o_a, o_b, o_c, w_a, w_b, w_c, w_out):
    bsz, s, d = x.shape
    tm = min(TM_MERGE, s)
    lay = lambda shape: pl.BlockSpec((1,) + shape, lambda b, i: (layer,) + (0,) * len(shape))
    tok = lambda n: pl.BlockSpec((1, tm, n), lambda b, i: (b, i, 0))
    return pl.pallas_call(
        _merge_kernel,
        out_shape=jax.ShapeDtypeStruct((bsz, s, d), F32),
        grid=(bsz, s // tm),
        in_specs=[
            tok(d),
            lay((8, mod.shape[2])),
            lay((1, d)), lay((d, _W_GATES)),
            tok(GLA_V), tok(S5_WIDTH), tok(DIFF_HEADS * DIFF_DV),
            lay((GLA_V, d)), lay((S5_WIDTH, d)), lay((DIFF_HEADS * DIFF_DV, d)), lay((d, d)),
        ],
        out_specs=tok(d),
        compiler_params=pltpu.CompilerParams(
            dimension_semantics=("arbitrary", "arbitrary"), vmem_limit_bytes=56 * 1024 * 1024),
        name="merge",
    )(x, mod, g, w_all, o_a, o_b, o_c, w_a, w_b, w_c, w_out)


_HALO = 16


def _ffn_kernel(x_ref, mod_ref, g_ref, wup_ref, cw_ref, cb_ref, wd_ref,
                o_ref, h_scr, halo_scr, up_scr, act_scr, *, tm, tf, f):
    b = pl.program_id(0)

    @pl.when(pl.program_id(1) == 0)
    def _():
        halo_scr[...] = jnp.zeros_like(halo_scr)

    x = x_ref[0]
    h = _norm_mod(x, g_ref[0], _mod_rows(mod_ref, b, 4), _mod_rows(mod_ref, b, 3)).astype(BF16)
    h_scr[0:_HALO, :] = halo_scr[...]
    h_scr[_HALO:, :] = h
    halo_scr[...] = h[tm - _HALO:, :]
    hh = h_scr[...]

    def conv(part, col):
        cw = cw_ref[0, :, col:col + tf]
        taps = [cw[n:n + 1, :] * up_scr[part, pl.ds(_HALO - 2 + n, tm), :] for n in range(3)]
        return taps[0] + taps[1] + taps[2] + cb_ref[0, :, col:col + tf]

    for c in range(f // tf):
        for part in range(2):
            col = part * f + c * tf
            up_scr[part] = _dot(hh, wup_ref[0, :, col:col + tf])
        a = conv(0, c * tf)
        gg = conv(1, f + c * tf)
        act_scr[:, c * tf:(c + 1) * tf] = (a * jax.nn.sigmoid(a) * gg).astype(BF16)

    o_ref[0] = x + _mod_rows(mod_ref, b, 5) * _dot(act_scr[...], wd_ref[0])


def _ffn(x, mod, layer, g, w_up, conv_w, conv_b, w_down):
    bsz, s, d = x.shape
    f = w_down.shape[1]
    tm = min(TM_FFN, s)
    tf = TF
    resident = lambda shape: pl.BlockSpec((1,) + shape, lambda b, i: (layer,) + (0,) * len(shape),
                                          pipeline_mode=pl.Buffered(1))
    return pl.pallas_call(
        functools.partial(_ffn_kernel, tm=tm, tf=tf, f=f),
        out_shape=jax.ShapeDtypeStruct((bsz, s, d), F32),
        grid=(bsz, s // tm),
        in_specs=[
            pl.BlockSpec((1, tm, d), lambda b, i: (b, i, 0)),
            pl.BlockSpec((1, 8, mod.shape[2]), lambda b, i: (layer, 0, 0)),
            resident((1, d)), resident((d, 2 * f)), resident((3, 2 * f)), resident((1, 2 * f)),
            resident((f, d)),
        ],
        out_specs=pl.BlockSpec((1, tm, d), lambda b, i: (b, i, 0)),
        scratch_shapes=[
            pltpu.VMEM((tm + _HALO, d), BF16), pltpu.VMEM((_HALO, d), BF16),
            pltpu.VMEM((2, tm + _HALO, tf), F32), pltpu.VMEM((tm, f), BF16)],
        compiler_params=pltpu.CompilerParams(
            dimension_semantics=("arbitrary", "arbitrary"), vmem_limit_bytes=56 * 1024 * 1024),
        name="ffn",
    )(x, mod, g, w_up, conv_w, conv_b, w_down)


def kernel(x, c, ada_w, ada_b, norm1_g, w_in, gla_gk_w, gla_gk_b, gla_norm_g, s5_lambda_re, s5_lambda_im, s5_b_re, s5_b_im, s5_c_re, s5_c_im, s5_d, s5_log_dt, s5_glu_w, s5_glu_b, diff_q_norm_g, diff_k_norm_g, diff_lambda_q1, diff_lambda_k1, diff_lambda_q2, diff_lambda_k2, diff_subln_g, w_branch_gla, w_branch_s5, w_branch_diff, w_out, norm2_g, ffn_w_up, ffn_conv_w, ffn_conv_b, ffn_w_down):
    depth, d = norm1_g.shape
    mod = _modulation(c, ada_w, ada_b)

    off_dv = _OFF_DQ + 2 * DIFF_HEADS * 2 * DIFF_DH
    w_all = jnp.concatenate(
        [w_in[:, :, _OFF_GATES:], w_in[:, :, _OFF_GLA:_OFF_GLR], w_in[:, :, _OFF_SU:_OFF_GATES]],
        axis=2).astype(BF16)
    w_lr = jnp.pad(w_in[:, :, _OFF_GLR:_OFF_SU], ((0, 0), (0, 0), (0, LANE - GLA_LR))).astype(BF16)
    w_vt = jnp.swapaxes(w_in[:, :, off_dv:_OFF_GATES], 1, 2).astype(BF16)
    w_a, w_b, w_c = (w.astype(BF16) for w in (w_branch_gla, w_branch_s5, w_branch_diff))
    w_o = w_out.astype(BF16)
    w_up, w_dn = ffn_w_up.astype(BF16), ffn_w_down.astype(BF16)
    glu_w = s5_glu_w.astype(BF16)
    g1, g2 = norm1_g.reshape(depth, 1, d), norm2_g.reshape(depth, 1, d)
    qg = jnp.tile(diff_q_norm_g, (1, 2)).reshape(depth, 1, LANE)
    kg = jnp.tile(diff_k_norm_g, (1, 2)).reshape(depth, 1, LANE)
    sg = diff_subln_g.reshape(depth, 1, DIFF_DV)
    lam_rows = jnp.pad(
        jnp.stack([diff_lambda_q1, diff_lambda_k1, diff_lambda_q2, diff_lambda_k2], axis=1),
        ((0, 0), (0, 4), (0, LANE - DIFF_DH)))
    seq = x.shape[1]
    s5_tabs = _s5_tables(s5_lambda_re, s5_lambda_im, s5_b_re, s5_b_im, s5_c_re, s5_c_im, s5_log_dt,
                         min(TS, seq) // _S5_SEG)
    s5_d3 = s5_d.reshape(depth, 1, S5_WIDTH)
    glu_b3 = s5_glu_b.reshape(depth, 1, S5_WIDTH)
    conv_b3 = ffn_conv_b.reshape(depth, 1, -1)

    for l in range(depth):
        lambda_init = 0.8 - 0.6 * math.exp(-0.3 * l)
        p_gla = _inproj_gla(x, mod, l, g1, w_all, w_lr)
        qku, vt = _inproj_qkuv(x, mod, l, g1, w_all, w_vt, qg, kg)
        o_a = _gla(p_gla, gla_gk_w[l], gla_gk_b[l], gla_norm_g[l])
        o_b = _s5(qku, l, s5_tabs, s5_d3, glu_w, glu_b3)
        o_c = _diff_attention(qku, vt, l, lam_rows, sg, lambda_init)
        x = _merge(x, mod, l, g1, w_all, o_a, o_b, o_c, w_a, w_b, w_c, w_o)
        x = _ffn(x, mod, l, g2, w_up, ffn_conv_w, conv_b3, w_dn)
    return x
```

```python
import functools
import math

import jax
import jax.numpy as jnp
import numpy as np
from jax import lax
from jax.experimental import pallas as pl
from jax.experimental.pallas import tpu as pltpu

F32 = jnp.float32
BF16 = jnp.bfloat16

D_MODEL = 1024
GLA_HEADS = 4
GLA_DK = 64
GLA_DV = 128
GLA_LR = 16
GLA_TAU = 16.0
GLA_CHUNK = 64
GLA_QK = GLA_HEADS * GLA_DK
GLA_V = GLA_HEADS * GLA_DV
GLA_COLS = 2 * GLA_QK + 2 * GLA_V + 128
S5_WIDTH = 512
S5_GROUP = 16
S5_GROUPS = 32
S5_STATE = 64
S5_LANES = S5_GROUPS * S5_STATE
DIFF_HEADS = 4
DIFF_DH = 64
DIFF_DV = 128
D_FF = 2816
EPS = 1e-6
LANE = 128
NEG_BIG = -1e30

TM_IN = 1024
TG = 256
TS = 512
TQ = 1024
TK = 512
TM_MERGE = 512
TM_FFN = 512
TF = 256

_OFF_GLA = 0
_OFF_GLR = 1536
_OFF_SU = 1552
_OFF_DQ = 2064
_OFF_GATES = 3600
_IN_COLS = 6672

_W_GATES = 3 * D_MODEL
_W_GLA = 2 * GLA_QK + 2 * GLA_V
_W_GLA_BLK = _W_GATES // _W_GLA
_W_TILE = 512
_W_U_BLK = (_W_GATES + _W_GLA) // _W_TILE


def _dot(a, b):
    return jnp.dot(a, b, preferred_element_type=F32)


def _dot_nt(a, b):
    return lax.dot_general(a, b, (((1,), (1,)), ((), ())), preferred_element_type=F32)


def _dot_tn(a, b):
    return lax.dot_general(a, b, (((0,), (0,)), ((), ())), preferred_element_type=F32)


def _norm_mod(x, g, sc, sh):
    ms = jnp.mean(x * x, axis=-1, keepdims=True)
    y = x * lax.rsqrt(ms + EPS) * g
    return y * (1.0 + sc) + sh


def _mod_rows(mod_ref, b, k):
    return mod_ref[0, pl.ds(b, 1), k * D_MODEL:(k + 1) * D_MODEL]


def _mod_kernel(c_ref, w_ref, b_ref, o_ref):
    c = c_ref[...]
    ca = (c * jax.nn.sigmoid(c)).astype(BF16)
    o_ref[0] = _dot(ca, w_ref[0].astype(BF16)) + b_ref[0]


def _modulation(c, ada_w, ada_b):
    depth, d, n = ada_w.shape
    bsz = c.shape[0]
    cp = jnp.zeros((8, d), F32).at[:bsz].set(c)
    tn = 1536
    return pl.pallas_call(
        _mod_kernel,
        out_shape=jax.ShapeDtypeStruct((depth, 8, n), F32),
        grid=(depth, n // tn),
        in_specs=[
            pl.BlockSpec((8, d), lambda l, j: (0, 0)),
            pl.BlockSpec((1, d, tn), lambda l, j: (l, 0, j)),
            pl.BlockSpec((1, 1, tn), lambda l, j: (l, 0, j)),
        ],
        out_specs=pl.BlockSpec((1, 8, tn), lambda l, j: (l, 0, j)),
        name="adaln_mod",
    )(cp, ada_w, ada_b.reshape(depth, 1, n))


def _seg_rmsnorm(a, g128, scale):
    lane = lax.broadcasted_iota(jnp.int32, (1, LANE), 1)
    lo = (lane < DIFF_DH).astype(F32)
    hi = 1.0 - lo
    outs = []
    for hb in range(a.shape[-1] // LANE):
        xh = a[:, hb * LANE:(hb + 1) * LANE]
        sq = xh * xh
        s_lo = jnp.sum(sq * lo, axis=-1, keepdims=True)
        s_hi = jnp.sum(sq * hi, axis=-1, keepdims=True)
        ms = (s_lo * lo + s_hi * hi) * (1.0 / DIFF_DH)
        outs.append(xh * lax.rsqrt(ms + EPS) * g128 * scale)
    return jnp.concatenate(outs, axis=-1)


def _inproj_gla_kernel(x_ref, mod_ref, g_ref, w_ref, wlr_ref, o_ref):
    b = pl.program_id(0)
    h = _norm_mod(x_ref[0], g_ref[0], _mod_rows(mod_ref, b, 1), _mod_rows(mod_ref, b, 0)).astype(BF16)
    o_ref[0, :, 0:_W_GLA] = _dot(h, w_ref[0])
    o_ref[0, :, _W_GLA:] = _dot(h, wlr_ref[0])


def _inproj_gla(x, mod, layer, g, w_all, w_lr):
    bsz, s, d = x.shape
    tm = min(TM_IN, s)
    return pl.pallas_call(
        _inproj_gla_kernel,
        out_shape=jax.ShapeDtypeStruct((bsz, s, GLA_COLS), F32),
        grid=(bsz, s // tm),
        in_specs=[
            pl.BlockSpec((1, tm, d), lambda b, i: (b, i, 0)),
            pl.BlockSpec((1, 8, mod.shape[2]), lambda b, i: (layer, 0, 0)),
            pl.BlockSpec((1, 1, d), lambda b, i: (layer, 0, 0)),
            pl.BlockSpec((1, d, _W_GLA), lambda b, i: (layer, 0, _W_GLA_BLK)),
            pl.BlockSpec((1, d, LANE), lambda b, i: (layer, 0, 0)),
        ],
        out_specs=pl.BlockSpec((1, tm, GLA_COLS), lambda b, i: (b, i, 0)),
        compiler_params=pltpu.CompilerParams(
            dimension_semantics=("arbitrary", "arbitrary"), vmem_limit_bytes=48 * 1024 * 1024),
        name="inproj_gla",
    )(x, mod, g, w_all, w_lr)


def _inproj_qkuv_kernel(x_ref, mod_ref, g_ref, wu_ref, wq_ref, wk_ref, wvt_ref, qg_ref, kg_ref, o_ref, vt_ref):
    b = pl.program_id(0)
    h = _norm_mod(x_ref[0], g_ref[0], _mod_rows(mod_ref, b, 1), _mod_rows(mod_ref, b, 0)).astype(BF16)
    tn = _W_TILE
    q = _seg_rmsnorm(_dot(h, wq_ref[0]), qg_ref[0], DIFF_DH ** -0.5 * math.log2(math.e))
    o_ref[0, :, 0:tn] = q.astype(o_ref.dtype)
    o_ref[0, :, tn:2 * tn] = _seg_rmsnorm(_dot(h, wk_ref[0]), kg_ref[0], 1.0).astype(o_ref.dtype)
    o_ref[0, :, 2 * tn:] = _dot(h, wu_ref[0]).astype(o_ref.dtype)
    vt = _dot_nt(wvt_ref[0], h).astype(vt_ref.dtype)
    tk = vt_ref.shape[3]
    for n in range(vt_ref.shape[1]):
        vt_ref[0, n] = vt[:, n * tk:(n + 1) * tk]


def _inproj_qkuv(x, mod, layer, g, w_all, w_vt, q_gain, k_gain):
    bsz, s, d = x.shape
    tm = min(TM_IN, s)
    tk = min(TK, s)
    tn = _W_TILE
    nv = w_vt.shape[1]
    resident = lambda shape, *idx: pl.BlockSpec((1,) + shape, lambda b, i: (layer,) + idx,
                                                pipeline_mode=pl.Buffered(1))
    return pl.pallas_call(
        _inproj_qkuv_kernel,
        out_shape=(jax.ShapeDtypeStruct((bsz, s, 3 * tn), BF16),
                   jax.ShapeDtypeStruct((bsz, s // tk, nv, tk), BF16)),
        grid=(bsz, s // tm),
        in_specs=[
            pl.BlockSpec((1, tm, d), lambda b, i: (b, i, 0)),
            pl.BlockSpec((1, 8, mod.shape[2]), lambda b, i: (layer, 0, 0)),
            resident((1, d), 0, 0),
            resident((d, tn), 0, _W_U_BLK), resident((d, tn), 0, _W_U_BLK + 1),
            resident((d, tn), 0, _W_U_BLK + 2),
            resident((nv, d), 0, 0), resident((1, LANE), 0, 0), resident((1, LANE), 0, 0),
        ],
        out_specs=(
            pl.BlockSpec((1, tm, 3 * tn), lambda b, i: (b, i, 0)),
            pl.BlockSpec((1, tm // tk, nv, tk), lambda b, i: (b, i, 0, 0)),
        ),
        compiler_params=pltpu.CompilerParams(
            dimension_semantics=("arbitrary", "arbitrary"), vmem_limit_bytes=48 * 1024 * 1024),
        name="inproj_qkuv",
    )(x, mod, g, w_all, w_all, w_all, w_vt, q_gain, k_gain)


def _gla_kernel(p_ref, gkw_ref, gkb_ref, ng_ref, ltri_ref, lall_ref, o_ref, s_scr, *, nchunk):
    @pl.when(pl.program_id(1) == 0)
    def _():
        s_scr[...] = jnp.zeros_like(s_scr)

    blk = p_ref[0]
    q = blk[:, 0:GLA_QK]
    k = blk[:, GLA_QK:2 * GLA_QK]
    v = blk[:, 2 * GLA_QK:2 * GLA_QK + GLA_V]
    r = blk[:, 2 * GLA_QK + GLA_V:2 * GLA_QK + 2 * GLA_V]
    lr = blk[:, 2 * GLA_QK + 2 * GLA_V:]
    z = _dot(lr.astype(BF16), gkw_ref[...]) + gkb_ref[...]
    la = (jnp.minimum(z, 0.0) - jnp.log1p(jnp.exp(-jnp.abs(z)))) * (1.0 / GLA_TAU)
    la_hi = la.astype(BF16)
    la_lo = (la - la_hi.astype(F32)).astype(BF16)
    ltri = ltri_ref[...]
    lall = lall_ref[...]
    bcum = _dot(ltri, la_hi) + _dot(ltri, la_lo)
    btot = _dot(lall, la_hi) + _dot(lall, la_lo)
    qd = q * (GLA_DK ** -0.5) * jnp.exp(bcum)
    kd = (k * jnp.exp(-bcum)).astype(BF16)
    ke = (k * jnp.exp(btot - bcum)).astype(BF16)
    vb = v.astype(BF16)
    gate = r * jax.nn.sigmoid(r)

    lane_head = lax.broadcasted_iota(jnp.int32, (1, GLA_QK), 1) // GLA_DK
    rowi = lax.broadcasted_iota(jnp.int32, (GLA_QK, GLA_CHUNK), 0) % GLA_CHUNK
    colj = lax.broadcasted_iota(jnp.int32, (GLA_QK, GLA_CHUNK), 1)
    tril = rowi >= colj
    ones_cv = jnp.ones((GLA_CHUNK, GLA_DV), BF16)
    ng = ng_ref[...]

    chunks = [slice(c * GLA_CHUNK, (c + 1) * GLA_CHUNK) for c in range(nchunk)]

    states = []
    state = s_scr[...]
    for rows in chunks:
        states.append(state.astype(BF16))
        kv = _dot_tn(ke[rows], vb[rows])
        kvd = jnp.concatenate(
            [kv[h * GLA_DK:(h + 1) * GLA_DK, h * GLA_DV:(h + 1) * GLA_DV] for h in range(GLA_HEADS)],
            axis=0)
        ldec = _dot_tn(la_hi[rows], ones_cv) + _dot_tn(la_lo[rows], ones_cv)
        state = jnp.exp(ldec) * state + kvd
    s_scr[...] = state

    qms = [jnp.concatenate(
        [jnp.where(lane_head == h, qd[rows], 0.0).astype(BF16) for h in range(GLA_HEADS)], axis=0)
        for rows in chunks]
    atts = [_dot_nt(qm, kd[rows]) for qm, rows in zip(qms, chunks)]
    inters = [_dot(qm, st) for qm, st in zip(qms, states)]
    atts = [jnp.where(tril, a, 0.0).astype(BF16) for a in atts]
    heads = [slice(h * GLA_CHUNK, (h + 1) * GLA_CHUNK) for h in range(GLA_HEADS)]
    outs = [[_dot(att[hr], vb[rows][:, h * GLA_DV:(h + 1) * GLA_DV]) + inter[hr]
             for h, hr in enumerate(heads)]
            for att, inter, rows in zip(atts, inters, chunks)]
    for o_heads, rows in zip(outs, chunks):
        normed = [o_h * lax.rsqrt(jnp.mean(o_h * o_h, axis=-1, keepdims=True) + EPS) * ng for o_h in o_heads]
        o = jnp.concatenate(normed, axis=1) * gate[rows]
        o_ref[0, rows, :] = o.astype(o_ref.dtype)


def _gla(p, gk_w, gk_b, norm_g):
    bsz, s, _ = p.shape
    tg = min(TG, s)
    nchunk = tg // GLA_CHUNK
    gkw = jnp.zeros((LANE, GLA_QK), F32).at[:GLA_LR].set(gk_w).astype(BF16)
    ri = jnp.arange(tg)[:, None]
    ci = jnp.arange(tg)[None, :]
    same = (ri // GLA_CHUNK) == (ci // GLA_CHUNK)
    ltri = (same & (ri >= ci)).astype(BF16)
    lall = same.astype(BF16)
    const = lambda shape: pl.BlockSpec(shape, lambda b, t: (0,) * len(shape))
    return pl.pallas_call(
        functools.partial(_gla_kernel, nchunk=nchunk),
        out_shape=jax.ShapeDtypeStruct((bsz, s, GLA_V), BF16),
        grid=(bsz, s // tg),
        in_specs=[
            pl.BlockSpec((1, tg, GLA_COLS), lambda b, t: (b, t, 0)),
            const((LANE, GLA_QK)), const((1, GLA_QK)), const((1, GLA_DV)),
            const((tg, tg)), const((tg, tg)),
        ],
        out_specs=pl.BlockSpec((1, tg, GLA_V), lambda b, t: (b, t, 0)),
        scratch_shapes=[pltpu.VMEM((GLA_QK, GLA_DV), F32)],
        compiler_params=pltpu.CompilerParams(dimension_semantics=("arbitrary", "arbitrary")),
        name="gla",
    )(p, gkw, gk_b.reshape(1, GLA_QK), norm_g.reshape(1, GLA_DV), ltri, lall)


_S5_COLS = S5_LANES // LANE
_S5_CG = 8
_S5_SEG = 8


def _s5_tables(lam_re, lam_im, b_re, b_im, c_re, c_im, log_dt, sl):
    nl = lam_re.shape[0]
    dt = jnp.exp(log_dt.astype(F32))[..., None]
    ar = (lam_re.astype(F32) * dt).reshape(nl, 1, S5_LANES)
    ai = (lam_im.astype(F32) * dt).reshape(nl, 1, S5_LANES)

    def powers(ks):
        kk = jnp.asarray(ks, F32)[None, :, None]
        mag = jnp.exp(kk * ar)
        return mag * jnp.cos(kk * ai), mag * jnp.sin(kk * ai)

    lr1, li1 = powers([1.0])
    lbr, lbi = lr1.reshape(lam_re.shape), li1.reshape(lam_re.shape)
    den = lam_re * lam_re + lam_im * lam_im
    nr = ((lbr - 1.0) * lam_re + lbi * lam_im) / den
    ni = (lbi * lam_re - (lbr - 1.0) * lam_im) / den
    bbr = nr[..., None] * b_re - ni[..., None] * b_im
    bbi = nr[..., None] * b_im + ni[..., None] * b_re
    eye4 = jnp.eye(4, dtype=F32)
    band = eye4[np.arange(8) % 4]

    def b_tiles(bb):
        bb4 = bb.reshape(nl, 8, 4, S5_STATE, S5_GROUP)
        small = jnp.einsum('lnkph,kj->lnkhjp', bb4, eye4).reshape(nl, 8, 64, 256)
        return jnp.einsum('lnrc,nq->lnqrc', small, band).reshape(nl, 8, 256, 256)

    def c_tiles(cc):
        cc4 = cc.reshape(nl, 8, 4, S5_GROUP, S5_STATE)
        small = jnp.einsum('lnkhp,kj->lnkpjh', cc4, eye4).reshape(nl, 8, 256, 64)
        return jnp.einsum('lnrc,nq->lnrqc', small, band).reshape(nl, 8, 256, 256)

    bt = jnp.concatenate([b_tiles(bbr), b_tiles(bbi)], axis=1).astype(BF16)
    ct = jnp.concatenate([c_tiles(c_re), c_tiles(-c_im)], axis=1).astype(BF16)
    lam8 = jnp.concatenate([jnp.broadcast_to(lr1, (nl, _S5_SEG, S5_LANES)),
                            jnp.broadcast_to(li1, (nl, _S5_SEG, S5_LANES))], axis=2)
    dsteps = np.array([1, 2, 4])
    keep = jnp.asarray(np.arange(_S5_SEG)[None, :] >= dsteps[:, None], F32)
    pr3, pi3 = powers(dsteps * float(sl))
    logstep = jnp.stack([keep[None, :, :, None] * pr3[:, :, None, :],
                         keep[None, :, :, None] * pi3[:, :, None, :]], axis=2)
    logstep = logstep.transpose(0, 3, 1, 2, 4).reshape(nl, _S5_SEG, 6 * S5_LANES)
    qr, qi = powers(np.arange(1, _S5_SEG + 1) * float(sl))
    seg_tab = jnp.concatenate([logstep, qr, qi], axis=2)
    pwr, pwi = powers(np.arange(1, sl + 1))
    pw = jnp.concatenate([pwr, pwi], axis=2)
    return bt, ct, lam8, seg_tab, pw


def _s5_kernel(u_ref, bt_ref, ct_ref, lam_ref, seg_ref, pw_ref, d_ref, gw_ref, gb_ref, o_ref,
               bu_scr, c_scr, x_scr, car_scr, *, ts):
    sl = ts // _S5_SEG
    pitch = sl + 8

    @pl.when(pl.program_id(1) == 0)
    def _():
        car_scr[...] = jnp.zeros_like(car_scr)

    u = u_ref[0]
    for t in range(16):
        half = (t % 8) // 4
        res = _dot(u[:, half * 256:(half + 1) * 256], bt_ref[0, t])
        for s in range(_S5_SEG):
            for w in range(2):
                bu_scr[2 * t + w, s * pitch:s * pitch + sl, :] = res[s * sl:(s + 1) * sl, w * LANE:(w + 1) * LANE]

    def lanes(ref, idx, j):
        return ref[0, :, idx * S5_LANES + j * LANE: idx * S5_LANES + (j + 1) * LANE]

    for j0 in range(0, _S5_COLS, _S5_CG):
        cols = list(range(j0, j0 + _S5_CG))

        def body(t, carry, cols=cols):
            new = [None] * (2 * len(cols))
            for n, j in enumerate(cols):
                xr, xi = carry[n], carry[len(cols) + n]
                lr_, li_ = lanes(lam_ref, 0, j), lanes(lam_ref, 1, j)
                vr = bu_scr[j, pl.ds(t, _S5_SEG, stride=pitch), :]
                vi = bu_scr[_S5_COLS + j, pl.ds(t, _S5_SEG, stride=pitch), :]
                nr = lr_ * xr - li_ * xi + vr
                ni = lr_ * xi + li_ * xr + vi
                bu_scr[j, pl.ds(t, _S5_SEG, stride=pitch), :] = nr
                bu_scr[_S5_COLS + j, pl.ds(t, _S5_SEG, stride=pitch), :] = ni
                new[n], new[len(cols) + n] = nr, ni
            return tuple(new)

        zero = jnp.zeros((_S5_SEG, LANE), F32)
        fin = lax.fori_loop(0, sl, body, (zero,) * (2 * len(cols)), unroll=4)

        row = lax.broadcasted_iota(jnp.int32, (_S5_SEG, LANE), 0)
        for n, j in enumerate(cols):
            gr, gi = fin[n], fin[len(cols) + n]
            for si, dstep in enumerate((1, 2, 4)):
                tr, ti = lanes(seg_ref, 2 * si, j), lanes(seg_ref, 2 * si + 1, j)
                sr = pltpu.roll(gr, dstep, 0)
                sm = pltpu.roll(gi, dstep, 0)
                gr, gi = gr + tr * sr - ti * sm, gi + tr * sm + ti * sr
            cr = car_scr[:, j * LANE:(j + 1) * LANE]
            ci = car_scr[:, S5_LANES + j * LANE:S5_LANES + (j + 1) * LANE]
            qr, qi = lanes(seg_ref, 6, j), lanes(seg_ref, 7, j)
            gr, gi = gr + qr * cr - qi * ci, gi + qr * ci + qi * cr
            c_scr[j] = jnp.where(row == 0, cr, pltpu.roll(gr, 1, 0))
            c_scr[_S5_COLS + j] = jnp.where(row == 0, ci, pltpu.roll(gi, 1, 0))
            car_scr[:, j * LANE:(j + 1) * LANE] = jnp.broadcast_to(gr[_S5_SEG - 1:, :], (_S5_SEG, LANE))
            car_scr[:, S5_LANES + j * LANE:S5_LANES + (j + 1) * LANE] = jnp.broadcast_to(
                gi[_S5_SEG - 1:, :], (_S5_SEG, LANE))

    for j in range(_S5_COLS):
        pr_ = pw_ref[0, :, j * LANE:(j + 1) * LANE]
        pi_ = pw_ref[0, :, S5_LANES + j * LANE:S5_LANES + (j + 1) * LANE]
        for s in range(_S5_SEG):
            cr = c_scr[j, s:s + 1, :]
            ci = c_scr[_S5_COLS + j, s:s + 1, :]
            xr = bu_scr[j, s * pitch:s * pitch + sl, :] + pr_ * cr - pi_ * ci
            xi = bu_scr[_S5_COLS + j, s * pitch:s * pitch + sl, :] + pr_ * ci + pi_ * cr
            x_scr[s * sl:(s + 1) * sl, j * LANE:(j + 1) * LANE] = xr.astype(BF16)
            x_scr[s * sl:(s + 1) * sl, S5_LANES + j * LANE:S5_LANES + (j + 1) * LANE] = xi.astype(BF16)

    ys = []
    for half in range(2):
        acc = None
        for t in [half * 4 + i for i in range(4)] + [8 + half * 4 + i for i in range(4)]:
            part = _dot(x_scr[:, t * 256:(t + 1) * 256], ct_ref[0, t])
            acc = part if acc is None else acc + part
        ys.append(acc)
    y = jnp.concatenate(ys, axis=1) + d_ref[0] * u.astype(F32)
    g = 0.5 * y * (1.0 + jnp.tanh(math.sqrt(2.0 / math.pi) * (y + 0.044715 * (y * y * y))))
    zz = _dot(g.astype(BF16), gw_ref[0]) + gb_ref[0]
    o_ref[0] = (g * jax.nn.sigmoid(zz)).astype(o_ref.dtype)


def _s5(qku, layer, tables, d_skip, glu_w, glu_b):
    bsz, s, _ = qku.shape
    ts = min(TS, s)
    sl = ts // _S5_SEG
    bt, ct, lam8, seg_tab, pw = tables
    const = lambda shape: pl.BlockSpec((1,) + shape, lambda b, t: (layer,) + (0,) * len(shape))
    return pl.pallas_call(
        functools.partial(_s5_kernel, ts=ts),
        out_shape=jax.ShapeDtypeStruct((bsz, s, S5_WIDTH), BF16),
        grid=(bsz, s // ts),
        in_specs=[
            pl.BlockSpec((1, ts, S5_WIDTH), lambda b, t: (b, t, 2)),
            const((16, 256, 256)), const((16, 256, 256)),
            const((_S5_SEG, 2 * S5_LANES)), const((_S5_SEG, 8 * S5_LANES)), const((sl, 2 * S5_LANES)),
            const((1, S5_WIDTH)), const((S5_WIDTH, S5_WIDTH)), const((1, S5_WIDTH)),
        ],
        out_specs=pl.BlockSpec((1, ts, S5_WIDTH), lambda b, t: (b, t, 0)),
        scratch_shapes=[
            pltpu.VMEM((2 * _S5_COLS, _S5_SEG * (sl + 8), LANE), F32),
            pltpu.VMEM((2 * _S5_COLS, _S5_SEG, LANE), F32),
            pltpu.VMEM((ts, 2 * S5_LANES), BF16),
            pltpu.VMEM((_S5_SEG, 2 * S5_LANES), F32)],
        compiler_params=pltpu.CompilerParams(
            dimension_semantics=("arbitrary", "arbitrary"), vmem_limit_bytes=48 * 1024 * 1024),
        name="s5",
    )(qku, bt, ct, lam8, seg_tab, pw, d_skip, glu_w, glu_b)


_ATT_STRIP = 256
_ATT_AHEAD = 3
_ATT_ONES = 16


def _alibi_tables(tq, tk):
    parts, rem = [], math.log2(math.e)
    for _ in range(3):
        p = float(np.asarray(rem, dtype=BF16))
        parts.append(p)
        rem -= p
    qa = ((np.arange(tq) // 64) * 64).astype(np.float32)
    qb = (np.arange(tq) % 64).astype(np.float32)
    ka = ((np.arange(tk) // 64) * 64).astype(np.float32)
    kb = (np.arange(tk) % 64).astype(np.float32)
    qx = np.zeros((DIFF_HEADS, tq, LANE), np.float32)
    kx = np.zeros((DIFF_HEADS, tk, LANE), np.float32)
    cval = np.zeros((DIFF_HEADS, 8, LANE), np.float32)
    for h in range(DIFF_HEADS):
        slope = 2.0 ** (-8.0 * (h + 1) / DIFF_HEADS)
        for n, p in enumerate(parts):
            qx[h, :, n] = qx[h, :, 3 + n] = slope * p
            kx[h, :, n] = ka
            kx[h, :, 3 + n] = kb
            qx[h, :, 6 + n] = -qa
            qx[h, :, 9 + n] = -qb
            kx[h, :, 6 + n] = kx[h, :, 9 + n] = slope * p
        cval[h] = slope * math.log2(math.e)
    return jnp.asarray(qx, BF16), jnp.asarray(kx, BF16), jnp.asarray(cval, F32)


def _attn_kernel(q_ref, k_ref, vt_ref, qx_ref, kx_ref, c_ref, lam_ref, sg_ref, o_ref,
                 qt_scr, m_scr, acc_scr, *, tq, tk, lambda_init):
    qi = pl.program_id(2)
    c11 = c_ref[0, 0:1, 0:1]
    qe = jnp.concatenate([q_ref[0], qx_ref[0]], axis=1).astype(F32)
    lane = lax.broadcasted_iota(jnp.int32, (1, 2 * LANE), 1)
    qt_scr[0] = jnp.where((lane < DIFF_DH) | (lane >= LANE), qe, 0.0).T.astype(BF16)
    qt_scr[1] = jnp.where(lane >= DIFF_DH, qe, 0.0).T.astype(BF16)
    kx = kx_ref[0]
    nstrip = tq // _ATT_STRIP
    kpq = tq // tk
    ones_rows = jnp.ones((_ATT_ONES, tk), BF16)

    m_scr[...] = jnp.full_like(m_scr, NEG_BIG)
    acc_scr[...] = jnp.zeros_like(acc_scr)

    def blocks(entries):
        loaded = []
        for kj, diag in entries:
            start = pl.multiple_of(kj * tk, tk)
            ke = jnp.concatenate([k_ref[0, pl.ds(start, tk), :], kx], axis=1)
            vt = jnp.concatenate([vt_ref[0, kj], ones_rows], axis=0)
            shift = c11 * (kj * tk - qi * tq).astype(F32)
            loaded.append((ke, vt, shift))
        units = []
        for e, (kj, diag) in enumerate(entries):
            for half in range(2):
                for st in range(nstrip):
                    nk = tk if diag is None else min(tk, (st + 1) * _ATT_STRIP - diag * tk)
                    if nk > 0:
                        units.append((e, half, st, nk))

        def scores(unit):
            e, half, st, nk = unit
            return _dot(loaded[e][0][:nk], qt_scr[half, :, st * _ATT_STRIP:(st + 1) * _ATT_STRIP])

        pending = [scores(u) for u in units[:_ATT_AHEAD]]
        for n, (e, half, st, nk) in enumerate(units):
            s_t = pending.pop(0)
            if n + _ATT_AHEAD < len(units):
                pending.append(scores(units[n + _ATT_AHEAD]))
            _, vt, shift = loaded[e]
            diag = entries[e][1]
            cols = slice(st * _ATT_STRIP, (st + 1) * _ATT_STRIP)
            if diag is not None and diag * tk + nk - 1 > st * _ATT_STRIP:
                jrow = lax.broadcasted_iota(jnp.int32, (nk, _ATT_STRIP), 0)
                icol = lax.broadcasted_iota(jnp.int32, (nk, _ATT_STRIP), 1)
                s_t = jnp.where(jrow + diag * tk <= icol + st * _ATT_STRIP, s_t, NEG_BIG)
            m_old = m_scr[half, :, cols] - shift
            m_new = jnp.maximum(m_old, jnp.max(s_t, axis=0, keepdims=True))
            alpha = jnp.exp2(m_old - m_new)
            p = jnp.exp2(s_t - m_new)
            acc_scr[half, :, cols] = alpha * acc_scr[half, :, cols] + _dot(vt[:, :nk], p.astype(BF16))
            m_scr[half, :, cols] = m_new + shift

    def body(i, carry):
        blocks([(i * kpq + d, None) for d in range(kpq)])
        return carry

    lax.fori_loop(0, qi, body, 0)
    blocks([(qi * kpq + d, d) for d in range(kpq)])

    lam = (jnp.exp(jnp.sum(lam_ref[0, 0:1, :] * lam_ref[0, 1:2, :], axis=-1, keepdims=True))
           - jnp.exp(jnp.sum(lam_ref[0, 2:3, :] * lam_ref[0, 3:4, :], axis=-1, keepdims=True)) + lambda_init)
    norm = [acc_scr[half, 0:DIFF_DV, :] / acc_scr[half, DIFF_DV:DIFF_DV + 1, :] for half in range(2)]
    o_t = norm[0] - lam * norm[1]
    o = o_t.T
    ms = jnp.mean(o * o, axis=-1, keepdims=True)
    o = o * lax.rsqrt(ms + EPS) * sg_ref[0] * (1.0 - lambda_init)
    o_ref[0] = o.astype(o_ref.dtype)


def _diff_attention(qku, vt, layer, lam_rows, subln_g, lambda_init):
    bsz, s, _ = qku.shape
    tk = vt.shape[3]
    tq = min(TQ, s)
    qx, kx, cval = _alibi_tables(tq, tk)
    return pl.pallas_call(
        functools.partial(_attn_kernel, tq=tq, tk=tk, lambda_init=lambda_init),
        out_shape=jax.ShapeDtypeStruct((bsz, s, DIFF_HEADS * DIFF_DV), BF16),
        grid=(bsz, DIFF_HEADS, s // tq),
        in_specs=[
            pl.BlockSpec((1, tq, LANE), lambda b, h, i: (b, i, h)),
            pl.BlockSpec((1, s, LANE), lambda b, h, i: (b, 0, DIFF_HEADS + h)),
            pl.BlockSpec((1, s // tk, DIFF_DV, tk), lambda b, h, i: (b, 0, h, 0)),
            pl.BlockSpec((1, tq, LANE), lambda b, h, i: (h, 0, 0)),
            pl.BlockSpec((1, tk, LANE), lambda b, h, i: (h, 0, 0)),
            pl.BlockSpec((1, 8, LANE), lambda b, h, i: (h, 0, 0)),
            pl.BlockSpec((1, 8, LANE), lambda b, h, i: (layer, 0, 0)),
            pl.BlockSpec((1, 1, LANE), lambda b, h, i: (layer, 0, 0)),
        ],
        out_specs=pl.BlockSpec((1, tq, LANE), lambda b, h, i: (b, i, h)),
        scratch_shapes=[
            pltpu.VMEM((2, 2 * LANE, tq), BF16), pltpu.VMEM((2, 1, tq), F32),
            pltpu.VMEM((2, DIFF_DV + _ATT_ONES, tq), F32)],
        compiler_params=pltpu.CompilerParams(
            dimension_semantics=("arbitrary", "arbitrary", "arbitrary"),
            vmem_limit_bytes=48 * 1024 * 1024),
        name="diff_attn",
    )(qku, qku, vt, qx, kx, cval, lam_rows, subln_g)


def _merge_kernel(x_ref, mod_ref, g_ref, wg_ref, oa_ref, ob_ref, oc_ref, wa_ref, wb_ref, wc_ref,
                  wo_ref, o_ref):
    b = pl.program_id(0)
    x = x_ref[0]
    h = _norm_mod(x, g_ref[0], _mod_rows(mod_ref, b, 1), _mod_rows(mod_ref, b, 0)).astype(BF16)
    merged = None
    for n, (ob, wb) in enumerate(((oa_ref, wa_ref), (ob_ref, wb_ref), (oc_ref, wc_ref))):
        gate = jax.nn.sigmoid(_dot(h, wg_ref[0, :, n * D_MODEL:(n + 1) * D_MODEL]))
        term = gate * _dot(ob[0], wb[0])
        merged = term if merged is None else merged + term
    y = _dot(merged.astype(BF16), wo_ref[0])
    o_ref[0] = x + _mod_rows(mod_ref, b, 2) * y


def _merge(x, mod, layer, g, w_all, o_a, o_b, o_c, w_a, w_b, w_c, w_out):
    bsz, s, d = x.shape
    tm = min(TM_MERGE, s)
    lay = lambda shape: pl.BlockSpec((1,) + shape, lambda b, i: (layer,) + (0,) * len(shape))
    tok = lambda n: pl.BlockSpec((1, tm, n), lambda b, i: (b, i, 0))
    return pl.pallas_call(
        _merge_kernel,
        out_shape=jax.ShapeDtypeStruct((bsz, s, d), F32),
        grid=(bsz, s // tm),
        in_specs=[
            tok(d),
            lay((8, mod.shape[2])),
            lay((1, d)), lay((d, _W_GATES)),
            tok(GLA_V), tok(S5_WIDTH), tok(DIFF_HEADS * DIFF_DV),
            lay((GLA_V, d)), lay((S5_WIDTH, d)), lay((DIFF_HEADS * DIFF_DV, d)), lay((d, d)),
        ],
        out_specs=tok(d),
        compiler_params=pltpu.CompilerParams(
            dimension_semantics=("arbitrary", "arbitrary"), vmem_limit_bytes=56 * 1024 * 1024),
        name="merge",
    )(x, mod, g, w_all, o_a, o_b, o_c, w_a, w_b, w_c, w_out)


_HALO = 16


def _ffn_kernel(x_ref, mod_ref, g_ref, wup_ref, cw_ref, cb_ref, wd_ref,
                o_ref, h_scr, halo_scr, up_scr, act_scr, *, tm, tf, f):
    b = pl.program_id(0)

    @pl.when(pl.program_id(1) == 0)
    def _():
        halo_scr[...] = jnp.zeros_like(halo_scr)

    x = x_ref[0]
    h = _norm_mod(x, g_ref[0], _mod_rows(mod_ref, b, 4), _mod_rows(mod_ref, b, 3)).astype(BF16)
    h_scr[0:_HALO, :] = halo_scr[...]
    h_scr[_HALO:, :] = h
    halo_scr[...] = h[tm - _HALO:, :]
    hh = h_scr[...]

    def conv(part, col):
        cw = cw_ref[0, :, col:col + tf]
        taps = [cw[n:n + 1, :] * up_scr[part, pl.ds(_HALO - 2 + n, tm), :] for n in range(3)]
        return taps[0] + taps[1] + taps[2] + cb_ref[0, :, col:col + tf]

    for c in range(f // tf):
        for part in range(2):
            col = part * f + c * tf
            up_scr[part] = _dot(hh, wup_ref[0, :, col:col + tf])
        a = conv(0, c * tf)
        gg = conv(1, f + c * tf)
        act_scr[:, c * tf:(c + 1) * tf] = (a * jax.nn.sigmoid(a) * gg).astype(BF16)

    o_ref[0] = x + _mod_rows(mod_ref, b, 5) * _dot(act_scr[...], wd_ref[0])


def _ffn(x, mod, layer, g, w_up, conv_w, conv_b, w_down):
    bsz, s, d = x.shape
    f = w_down.shape[1]
    tm = min(TM_FFN, s)
    tf = TF
    resident = lambda shape: pl.BlockSpec((1,) + shape, lambda b, i: (layer,) + (0,) * len(shape),
                                          pipeline_mode=pl.Buffered(1))
    return pl.pallas_call(
        functools.partial(_ffn_kernel, tm=tm, tf=tf, f=f),
        out_shape=jax.ShapeDtypeStruct((bsz, s, d), F32),
        grid=(bsz, s // tm),
        in_specs=[
            pl.BlockSpec((1, tm, d), lambda b, i: (b, i, 0)),
            pl.BlockSpec((1, 8, mod.shape[2]), lambda b, i: (layer, 0, 0)),
            resident((1, d)), resident((d, 2 * f)), resident((3, 2 * f)), resident((1, 2 * f)),
            resident((f, d)),
        ],
        out_specs=pl.BlockSpec((1, tm, d), lambda b, i: (b, i, 0)),
        scratch_shapes=[
            pltpu.VMEM((tm + _HALO, d), BF16), pltpu.VMEM((_HALO, d), BF16),
            pltpu.VMEM((2, tm + _HALO, tf), F32), pltpu.VMEM((tm, f), BF16)],
        compiler_params=pltpu.CompilerParams(
            dimension_semantics=("arbitrary", "arbitrary"), vmem_limit_bytes=56 * 1024 * 1024),
        name="ffn",
    )(x, mod, g, w_up, conv_w, conv_b, w_down)


def kernel(x, c, ada_w, ada_b, norm1_g, w_in, gla_gk_w, gla_gk_b, gla_norm_g, s5_lambda_re, s5_lambda_im, s5_b_re, s5_b_im, s5_c_re, s5_c_im, s5_d, s5_log_dt, s5_glu_w, s5_glu_b, diff_q_norm_g, diff_k_norm_g, diff_lambda_q1, diff_lambda_k1, diff_lambda_q2, diff_lambda_k2, diff_subln_g, w_branch_gla, w_branch_s5, w_branch_diff, w_out, norm2_g, ffn_w_up, ffn_conv_w, ffn_conv_b, ffn_w_down):
    depth, d = norm1_g.shape
    mod = _modulation(c, ada_w, ada_b)

    off_dv = _OFF_DQ + 2 * DIFF_HEADS * 2 * DIFF_DH
    w_all = jnp.concatenate(
        [w_in[:, :, _OFF_GATES:], w_in[:, :, _OFF_GLA:_OFF_GLR], w_in[:, :, _OFF_SU:_OFF_GATES]],
        axis=2).astype(BF16)
    w_lr = jnp.pad(w_in[:, :, _OFF_GLR:_OFF_SU], ((0, 0), (0, 0), (0, LANE - GLA_LR))).astype(BF16)
    w_vt = jnp.swapaxes(w_in[:, :, off_dv:_OFF_GATES], 1, 2).astype(BF16)
    w_a, w_b, w_c = (w.astype(BF16) for w in (w_branch_gla, w_branch_s5, w_branch_diff))
    w_o = w_out.astype(BF16)
    w_up, w_dn = ffn_w_up.astype(BF16), ffn_w_down.astype(BF16)
    glu_w = s5_glu_w.astype(BF16)
    g1, g2 = norm1_g.reshape(depth, 1, d), norm2_g.reshape(depth, 1, d)
    qg = jnp.tile(diff_q_norm_g, (1, 2)).reshape(depth, 1, LANE)
    kg = jnp.tile(diff_k_norm_g, (1, 2)).reshape(depth, 1, LANE)
    sg = diff_subln_g.reshape(depth, 1, DIFF_DV)
    lam_rows = jnp.pad(
        jnp.stack([diff_lambda_q1, diff_lambda_k1, diff_lambda_q2, diff_lambda_k2], axis=1),
        ((0, 0), (0, 4), (0, LANE - DIFF_DH)))
    seq = x.shape[1]
    s5_tabs = _s5_tables(s5_lambda_re, s5_lambda_im, s5_b_re, s5_b_im, s5_c_re, s5_c_im, s5_log_dt,
                         min(TS, seq) // _S5_SEG)
    s5_d3 = s5_d.reshape(depth, 1, S5_WIDTH)
    glu_b3 = s5_glu_b.reshape(depth, 1, S5_WIDTH)
    conv_b3 = ffn_conv_b.reshape(depth, 1, -1)

    for l in range(depth):
        lambda_init = 0.8 - 0.6 * math.exp(-0.3 * l)
        p_gla = _inproj_gla(x, mod, l, g1, w_all, w_lr)
        qku, vt = _inproj_qkuv(x, mod, l, g1, w_all, w_vt, qg, kg)
        o_a = _gla(p_gla, gla_gk_w[l], gla_gk_b[l], gla_norm_g[l])
        o_b = _s5(qku, l, s5_tabs, s5_d3, glu_w, glu_b3)
        o_c = _diff_attention(qku, vt, l, lam_rows, sg, lambda_init)
        x = _merge(x, mod, l, g1, w_all, o_a, o_b, o_c, w_a, w_b, w_c, w_o)
        x = _ffn(x, mod, l, g2, w_up, ffn_conv_w, conv_b3, w_dn)
    return x
```

```python
import functools
import math

import jax
import jax.numpy as jnp
import numpy as np
from jax import lax
from jax.experimental import pallas as pl
from jax.experimental.pallas import tpu as pltpu

F32 = jnp.float32
BF16 = jnp.bfloat16

D_MODEL = 1024
GLA_HEADS = 4
GLA_DK = 64
GLA_DV = 128
GLA_LR = 16
GLA_TAU = 16.0
GLA_CHUNK = 64
GLA_QK = GLA_HEADS * GLA_DK
GLA_V = GLA_HEADS * GLA_DV
GLA_COLS = 2 * GLA_QK + 2 * GLA_V + 128
S5_WIDTH = 512
S5_GROUP = 16
S5_GROUPS = 32
S5_STATE = 64
S5_LANES = S5_GROUPS * S5_STATE
DIFF_HEADS = 4
DIFF_DH = 64
DIFF_DV = 128
D_FF = 2816
EPS = 1e-6
LANE = 128
NEG_BIG = -1e30

TM_IN = 1024
TG = 256
TS = 512
TQ = 1024
TK = 512
TM_MERGE = 512
TM_FFN = 512
TF = 256

_OFF_GLA = 0
_OFF_GLR = 1536
_OFF_SU = 1552
_OFF_DQ = 2064
_OFF_GATES = 3600
_IN_COLS = 6672

_W_GATES = 3 * D_MODEL
_W_GLA = 2 * GLA_QK + 2 * GLA_V
_W_GLA_BLK = _W_GATES // _W_GLA
_W_TILE = 512
_W_U_BLK = (_W_GATES + _W_GLA) // _W_TILE


def _dot(a, b):
    return jnp.dot(a, b, preferred_element_type=F32)


def _dot_nt(a, b):
    return lax.dot_general(a, b, (((1,), (1,)), ((), ())), preferred_element_type=F32)


def _dot_tn(a, b):
    return lax.dot_general(a, b, (((0,), (0,)), ((), ())), preferred_element_type=F32)


def _norm_mod(x, g, sc, sh):
    ms = jnp.mean(x * x, axis=-1, keepdims=True)
    y = x * lax.rsqrt(ms + EPS) * g
    return y * (1.0 + sc) + sh


def _mod_rows(mod_ref, b, k):
    return mod_ref[0, pl.ds(b, 1), k * D_MODEL:(k + 1) * D_MODEL]


def _mod_kernel(c_ref, w_ref, b_ref, o_ref):
    c = c_ref[...]
    ca = (c * jax.nn.sigmoid(c)).astype(BF16)
    o_ref[0] = _dot(ca, w_ref[0].astype(BF16)) + b_ref[0]


def _modulation(c, ada_w, ada_b):
    depth, d, n = ada_w.shape
    bsz = c.shape[0]
    cp = jnp.zeros((8, d), F32).at[:bsz].set(c)
    tn = 1536
    return pl.pallas_call(
        _mod_kernel,
        out_shape=jax.ShapeDtypeStruct((depth, 8, n), F32),
        grid=(depth, n // tn),
        in_specs=[
            pl.BlockSpec((8, d), lambda l, j: (0, 0)),
            pl.BlockSpec((1, d, tn), lambda l, j: (l, 0, j)),
            pl.BlockSpec((1, 1, tn), lambda l, j: (l, 0, j)),
        ],
        out_specs=pl.BlockSpec((1, 8, tn), lambda l, j: (l, 0, j)),
        name="adaln_mod",
    )(cp, ada_w, ada_b.reshape(depth, 1, n))


def _seg_rmsnorm(a, g128, scale):
    lane = lax.broadcasted_iota(jnp.int32, (1, LANE), 1)
    lo = (lane < DIFF_DH).astype(F32)
    hi = 1.0 - lo
    outs = []
    for hb in range(a.shape[-1] // LANE):
        xh = a[:, hb * LANE:(hb + 1) * LANE]
        sq = xh * xh
        s_lo = jnp.sum(sq * lo, axis=-1, keepdims=True)
        s_hi = jnp.sum(sq * hi, axis=-1, keepdims=True)
        ms = (s_lo * lo + s_hi * hi) * (1.0 / DIFF_DH)
        outs.append(xh * lax.rsqrt(ms + EPS) * g128 * scale)
    return jnp.concatenate(outs, axis=-1)


def _inproj_gla_kernel(x_ref, mod_ref, g_ref, w_ref, wlr_ref, o_ref):
    b = pl.program_id(0)
    h = _norm_mod(x_ref[0], g_ref[0], _mod_rows(mod_ref, b, 1), _mod_rows(mod_ref, b, 0)).astype(BF16)
    o_ref[0, :, 0:_W_GLA] = _dot(h, w_ref[0])
    o_ref[0, :, _W_GLA:] = _dot(h, wlr_ref[0])


def _inproj_gla(x, mod, layer, g, w_all, w_lr):
    bsz, s, d = x.shape
    tm = min(TM_IN, s)
    return pl.pallas_call(
        _inproj_gla_kernel,
        out_shape=jax.ShapeDtypeStruct((bsz, s, GLA_COLS), F32),
        grid=(bsz, s // tm),
        in_specs=[
            pl.BlockSpec((1, tm, d), lambda b, i: (b, i, 0)),
            pl.BlockSpec((1, 8, mod.shape[2]), lambda b, i: (layer, 0, 0)),
            pl.BlockSpec((1, 1, d), lambda b, i: (layer, 0, 0)),
            pl.BlockSpec((1, d, _W_GLA), lambda b, i: (layer, 0, _W_GLA_BLK)),
            pl.BlockSpec((1, d, LANE), lambda b, i: (layer, 0, 0)),
        ],
        out_specs=pl.BlockSpec((1, tm, GLA_COLS), lambda b, i: (b, i, 0)),
        compiler_params=pltpu.CompilerParams(
            dimension_semantics=("arbitrary", "arbitrary"), vmem_limit_bytes=48 * 1024 * 1024),
        name="inproj_gla",
    )(x, mod, g, w_all, w_lr)


def _inproj_qkuv_kernel(x_ref, mod_ref, g_ref, wu_ref, wk_ref, wqvt_ref, kg_ref, o_ref, qt_ref, vt_ref):
    b = pl.program_id(0)
    h = _norm_mod(x_ref[0], g_ref[0], _mod_rows(mod_ref, b, 1), _mod_rows(mod_ref, b, 0)).astype(BF16)
    tn = _W_TILE
    o_ref[0, :, 0:tn] = _seg_rmsnorm(_dot(h, wk_ref[0]), kg_ref[0], 1.0).astype(o_ref.dtype)
    o_ref[0, :, tn:] = _dot(h, wu_ref[0]).astype(o_ref.dtype)
    qvt = _dot_nt(wqvt_ref[0], h)
    qscale = DIFF_DH ** -0.5 * math.log2(math.e)
    for seg in range(tn // DIFF_DH):
        rows = slice(seg * DIFF_DH, (seg + 1) * DIFF_DH)
        qs = qvt[rows]
        ms = jnp.mean(qs * qs, axis=0, keepdims=True)
        qt_ref[0, 0, rows, :] = (qs * (lax.rsqrt(ms + EPS) * qscale)).astype(qt_ref.dtype)
    vt = qvt[tn:].astype(vt_ref.dtype)
    tk = vt_ref.shape[3]
    for n in range(vt_ref.shape[1]):
        vt_ref[0, n] = vt[:, n * tk:(n + 1) * tk]


def _inproj_qkuv(x, mod, layer, g, w_all, w_qvt, k_gain):
    bsz, s, d = x.shape
    tm = min(TQ, s)
    tk = min(TK, s)
    tn = _W_TILE
    resident = lambda shape, *idx: pl.BlockSpec((1,) + shape, lambda b, i: (layer,) + idx,
                                                pipeline_mode=pl.Buffered(1))
    return pl.pallas_call(
        _inproj_qkuv_kernel,
        out_shape=(jax.ShapeDtypeStruct((bsz, s, 2 * tn), BF16),
                   jax.ShapeDtypeStruct((bsz, s // tm, tn, tm), BF16),
                   jax.ShapeDtypeStruct((bsz, s // tk, tn, tk), BF16)),
        grid=(bsz, s // tm),
        in_specs=[
            pl.BlockSpec((1, tm, d), lambda b, i: (b, i, 0)),
            pl.BlockSpec((1, 8, mod.shape[2]), lambda b, i: (layer, 0, 0)),
            resident((1, d), 0, 0),
            resident((d, tn), 0, _W_U_BLK), resident((d, tn), 0, _W_U_BLK + 2),
            resident((2 * tn, d), 0, 0), resident((1, LANE), 0, 0),
        ],
        out_specs=(
            pl.BlockSpec((1, tm, 2 * tn), lambda b, i: (b, i, 0)),
            pl.BlockSpec((1, 1, tn, tm), lambda b, i: (b, i, 0, 0)),
            pl.BlockSpec((1, tm // tk, tn, tk), lambda b, i: (b, i, 0, 0)),
        ),
        compiler_params=pltpu.CompilerParams(
            dimension_semantics=("arbitrary", "arbitrary"), vmem_limit_bytes=48 * 1024 * 1024),
        name="inproj_qkuv",
    )(x, mod, g, w_all, w_all, w_qvt, k_gain)


def _gla_kernel(p_ref, gkw_ref, gkb_ref, ng_ref, ltri_ref, lall_ref, o_ref, s_scr, *, nchunk):
    @pl.when(pl.program_id(1) == 0)
    def _():
        s_scr[...] = jnp.zeros_like(s_scr)

    blk = p_ref[0]
    q = blk[:, 0:GLA_QK]
    k = blk[:, GLA_QK:2 * GLA_QK]
    v = blk[:, 2 * GLA_QK:2 * GLA_QK + GLA_V]
    r = blk[:, 2 * GLA_QK + GLA_V:2 * GLA_QK + 2 * GLA_V]
    lr = blk[:, 2 * GLA_QK + 2 * GLA_V:]
    z = _dot(lr.astype(BF16), gkw_ref[...]) + gkb_ref[...]
    la = (jnp.minimum(z, 0.0) - jnp.log1p(jnp.exp(-jnp.abs(z)))) * (1.0 / GLA_TAU)
    la_hi = la.astype(BF16)
    la_lo = (la - la_hi.astype(F32)).astype(BF16)
    ltri = ltri_ref[...]
    lall = lall_ref[...]
    bcum = _dot(ltri, la_hi) + _dot(ltri, la_lo)
    btot = _dot(lall, la_hi) + _dot(lall, la_lo)
    qd = q * (GLA_DK ** -0.5) * jnp.exp(bcum)
    kd = (k * jnp.exp(-bcum)).astype(BF16)
    ke = (k * jnp.exp(btot - bcum)).astype(BF16)
    vb = v.astype(BF16)
    gate = r * jax.nn.sigmoid(r)

    lane_head = lax.broadcasted_iota(jnp.int32, (1, GLA_QK), 1) // GLA_DK
    rowi = lax.broadcasted_iota(jnp.int32, (GLA_QK, GLA_CHUNK), 0) % GLA_CHUNK
    colj = lax.broadcasted_iota(jnp.int32, (GLA_QK, GLA_CHUNK), 1)
    tril = rowi >= colj
    ones_cv = jnp.ones((GLA_CHUNK, GLA_DV), BF16)
    ng = ng_ref[...]

    chunks = [slice(c * GLA_CHUNK, (c + 1) * GLA_CHUNK) for c in range(nchunk)]

    states = []
    state = s_scr[...]
    for rows in chunks:
        states.append(state.astype(BF16))
        kv = _dot_tn(ke[rows], vb[rows])
        kvd = jnp.concatenate(
            [kv[h * GLA_DK:(h + 1) * GLA_DK, h * GLA_DV:(h + 1) * GLA_DV] for h in range(GLA_HEADS)],
            axis=0)
        ldec = _dot_tn(la_hi[rows], ones_cv) + _dot_tn(la_lo[rows], ones_cv)
        state = jnp.exp(ldec) * state + kvd
    s_scr[...] = state

    qms = [jnp.concatenate(
        [jnp.where(lane_head == h, qd[rows], 0.0).astype(BF16) for h in range(GLA_HEADS)], axis=0)
        for rows in chunks]
    atts = [_dot_nt(qm, kd[rows]) for qm, rows in zip(qms, chunks)]
    inters = [_dot(qm, st) for qm, st in zip(qms, states)]
    atts = [jnp.where(tril, a, 0.0).astype(BF16) for a in atts]
    heads = [slice(h * GLA_CHUNK, (h + 1) * GLA_CHUNK) for h in range(GLA_HEADS)]
    outs = [[_dot(att[hr], vb[rows][:, h * GLA_DV:(h + 1) * GLA_DV]) + inter[hr]
             for h, hr in enumerate(heads)]
            for att, inter, rows in zip(atts, inters, chunks)]
    for o_heads, rows in zip(outs, chunks):
        normed = [o_h * lax.rsqrt(jnp.mean(o_h * o_h, axis=-1, keepdims=True) + EPS) * ng for o_h in o_heads]
        o = jnp.concatenate(normed, axis=1) * gate[rows]
        o_ref[0, rows, :] = o.astype(o_ref.dtype)


def _gla(p, gk_w, gk_b, norm_g):
    bsz, s, _ = p.shape
    tg = min(TG, s)
    nchunk = tg // GLA_CHUNK
    gkw = jnp.zeros((LANE, GLA_QK), F32).at[:GLA_LR].set(gk_w).astype(BF16)
    ri = jnp.arange(tg)[:, None]
    ci = jnp.arange(tg)[None, :]
    same = (ri // GLA_CHUNK) == (ci // GLA_CHUNK)
    ltri = (same & (ri >= ci)).astype(BF16)
    lall = same.astype(BF16)
    const = lambda shape: pl.BlockSpec(shape, lambda b, t: (0,) * len(shape))
    return pl.pallas_call(
        functools.partial(_gla_kernel, nchunk=nchunk),
        out_shape=jax.ShapeDtypeStruct((bsz, s, GLA_V), BF16),
        grid=(bsz, s // tg),
        in_specs=[
            pl.BlockSpec((1, tg, GLA_COLS), lambda b, t: (b, t, 0)),
            const((LANE, GLA_QK)), const((1, GLA_QK)), const((1, GLA_DV)),
            const((tg, tg)), const((tg, tg)),
        ],
        out_specs=pl.BlockSpec((1, tg, GLA_V), lambda b, t: (b, t, 0)),
        scratch_shapes=[pltpu.VMEM((GLA_QK, GLA_DV), F32)],
        compiler_params=pltpu.CompilerParams(dimension_semantics=("arbitrary", "arbitrary")),
        name="gla",
    )(p, gkw, gk_b.reshape(1, GLA_QK), norm_g.reshape(1, GLA_DV), ltri, lall)


_S5_COLS = S5_LANES // LANE
_S5_CG = 8
_S5_SEG = 8


def _s5_tables(lam_re, lam_im, b_re, b_im, c_re, c_im, log_dt, sl):
    nl = lam_re.shape[0]
    dt = jnp.exp(log_dt.astype(F32))[..., None]
    ar = (lam_re.astype(F32) * dt).reshape(nl, 1, S5_LANES)
    ai = (lam_im.astype(F32) * dt).reshape(nl, 1, S5_LANES)

    def powers(ks):
        kk = jnp.asarray(ks, F32)[None, :, None]
        mag = jnp.exp(kk * ar)
        return mag * jnp.cos(kk * ai), mag * jnp.sin(kk * ai)

    lr1, li1 = powers([1.0])
    lbr, lbi = lr1.reshape(lam_re.shape), li1.reshape(lam_re.shape)
    den = lam_re * lam_re + lam_im * lam_im
    nr = ((lbr - 1.0) * lam_re + lbi * lam_im) / den
    ni = (lbi * lam_re - (lbr - 1.0) * lam_im) / den
    bbr = nr[..., None] * b_re - ni[..., None] * b_im
    bbi = nr[..., None] * b_im + ni[..., None] * b_re
    eye4 = jnp.eye(4, dtype=F32)
    band = eye4[np.arange(8) % 4]

    def b_tiles(bb):
        bb4 = bb.reshape(nl, 8, 4, S5_STATE, S5_GROUP)
        small = jnp.einsum('lnkph,kj->lnkhjp', bb4, eye4).reshape(nl, 8, 64, 256)
        return jnp.einsum('lnrc,nq->lnqrc', small, band).reshape(nl, 8, 256, 256)

    def c_tiles(cc):
        cc4 = cc.reshape(nl, 8, 4, S5_GROUP, S5_STATE)
        small = jnp.einsum('lnkhp,kj->lnkpjh', cc4, eye4).reshape(nl, 8, 256, 64)
        return jnp.einsum('lnrc,nq->lnrqc', small, band).reshape(nl, 8, 256, 256)

    bt = jnp.concatenate([b_tiles(bbr), b_tiles(bbi)], axis=1).astype(BF16)
    ct = jnp.concatenate([c_tiles(c_re), c_tiles(-c_im)], axis=1).astype(BF16)
    lam8 = jnp.concatenate([jnp.broadcast_to(lr1, (nl, _S5_SEG, S5_LANES)),
                            jnp.broadcast_to(li1, (nl, _S5_SEG, S5_LANES))], axis=2)
    dsteps = np.array([1, 2, 4])
    keep = jnp.asarray(np.arange(_S5_SEG)[None, :] >= dsteps[:, None], F32)
    pr3, pi3 = powers(dsteps * float(sl))
    logstep = jnp.stack([keep[None, :, :, None] * pr3[:, :, None, :],
                         keep[None, :, :, None] * pi3[:, :, None, :]], axis=2)
    logstep = logstep.transpose(0, 3, 1, 2, 4).reshape(nl, _S5_SEG, 6 * S5_LANES)
    qr, qi = powers(np.arange(1, _S5_SEG + 1) * float(sl))
    seg_tab = jnp.concatenate([logstep, qr, qi], axis=2)
    pwr, pwi = powers(np.arange(1, sl + 1))
    pw = jnp.concatenate([pwr, pwi], axis=2)
    return bt, ct, lam8, seg_tab, pw


def _s5_kernel(u_ref, bt_ref, ct_ref, lam_ref, seg_ref, pw_ref, d_ref, gw_ref, gb_ref, o_ref,
               bu_scr, c_scr, x_scr, car_scr, *, ts):
    sl = ts // _S5_SEG
    pitch = sl + 8

    @pl.when(pl.program_id(1) == 0)
    def _():
        car_scr[...] = jnp.zeros_like(car_scr)

    u = u_ref[0]
    for t in range(16):
        half = (t % 8) // 4
        res = _dot(u[:, half * 256:(half + 1) * 256], bt_ref[0, t])
        for s in range(_S5_SEG):
            for w in range(2):
                bu_scr[2 * t + w, s * pitch:s * pitch + sl, :] = res[s * sl:(s + 1) * sl, w * LANE:(w + 1) * LANE]

    def lanes(ref, idx, j):
        return ref[0, :, idx * S5_LANES + j * LANE: idx * S5_LANES + (j + 1) * LANE]

    for j0 in range(0, _S5_COLS, _S5_CG):
        cols = list(range(j0, j0 + _S5_CG))

        def body(t, carry, cols=cols):
            new = [None] * (2 * len(cols))
            for n, j in enumerate(cols):
                xr, xi = carry[n], carry[len(cols) + n]
                lr_, li_ = lanes(lam_ref, 0, j), lanes(lam_ref, 1, j)
                vr = bu_scr[j, pl.ds(t, _S5_SEG, stride=pitch), :]
                vi = bu_scr[_S5_COLS + j, pl.ds(t, _S5_SEG, stride=pitch), :]
                nr = lr_ * xr - li_ * xi + vr
                ni = lr_ * xi + li_ * xr + vi
                bu_scr[j, pl.ds(t, _S5_SEG, stride=pitch), :] = nr
                bu_scr[_S5_COLS + j, pl.ds(t, _S5_SEG, stride=pitch), :] = ni
                new[n], new[len(cols) + n] = nr, ni
            return tuple(new)

        zero = jnp.zeros((_S5_SEG, LANE), F32)
        fin = lax.fori_loop(0, sl, body, (zero,) * (2 * len(cols)), unroll=4)

        row = lax.broadcasted_iota(jnp.int32, (_S5_SEG, LANE), 0)
        for n, j in enumerate(cols):
            gr, gi = fin[n], fin[len(cols) + n]
            for si, dstep in enumerate((1, 2, 4)):
                tr, ti = lanes(seg_ref, 2 * si, j), lanes(seg_ref, 2 * si + 1, j)
                sr = pltpu.roll(gr, dstep, 0)
                sm = pltpu.roll(gi, dstep, 0)
                gr, gi = gr + tr * sr - ti * sm, gi + tr * sm + ti * sr
            cr = car_scr[:, j * LANE:(j + 1) * LANE]
            ci = car_scr[:, S5_LANES + j * LANE:S5_LANES + (j + 1) * LANE]
            qr, qi = lanes(seg_ref, 6, j), lanes(seg_ref, 7, j)
            gr, gi = gr + qr * cr - qi * ci, gi + qr * ci + qi * cr
            c_scr[j] = jnp.where(row == 0, cr, pltpu.roll(gr, 1, 0))
            c_scr[_S5_COLS + j] = jnp.where(row == 0, ci, pltpu.roll(gi, 1, 0))
            car_scr[:, j * LANE:(j + 1) * LANE] = jnp.broadcast_to(gr[_S5_SEG - 1:, :], (_S5_SEG, LANE))
            car_scr[:, S5_LANES + j * LANE:S5_LANES + (j + 1) * LANE] = jnp.broadcast_to(
                gi[_S5_SEG - 1:, :], (_S5_SEG, LANE))

    for j in range(_S5_COLS):
        pr_ = pw_ref[0, :, j * LANE:(j + 1) * LANE]
        pi_ = pw_ref[0, :, S5_LANES + j * LANE:S5_LANES + (j + 1) * LANE]
        for s in range(_S5_SEG):
            cr = c_scr[j, s:s + 1, :]
            ci = c_scr[_S5_COLS + j, s:s + 1, :]
            xr = bu_scr[j, s * pitch:s * pitch + sl, :] + pr_ * cr - pi_ * ci
            xi = bu_scr[_S5_COLS + j, s * pitch:s * pitch + sl, :] + pr_ * ci + pi_ * cr
            x_scr[s * sl:(s + 1) * sl, j * LANE:(j + 1) * LANE] = xr.astype(BF16)
            x_scr[s * sl:(s + 1) * sl, S5_LANES + j * LANE:S5_LANES + (j + 1) * LANE] = xi.astype(BF16)

    ys = []
    for half in range(2):
        acc = None
        for t in [half * 4 + i for i in range(4)] + [8 + half * 4 + i for i in range(4)]:
            part = _dot(x_scr[:, t * 256:(t + 1) * 256], ct_ref[0, t])
            acc = part if acc is None else acc + part
        ys.append(acc)
    y = jnp.concatenate(ys, axis=1) + d_ref[0] * u.astype(F32)
    g = 0.5 * y * (1.0 + jnp.tanh(math.sqrt(2.0 / math.pi) * (y + 0.044715 * (y * y * y))))
    zz = _dot(g.astype(BF16), gw_ref[0]) + gb_ref[0]
    o_ref[0] = (g * jax.nn.sigmoid(zz)).astype(o_ref.dtype)


def _s5(qku, layer, tables, d_skip, glu_w, glu_b):
    bsz, s, _ = qku.shape
    ts = min(TS, s)
    sl = ts // _S5_SEG
    bt, ct, lam8, seg_tab, pw = tables
    const = lambda shape: pl.BlockSpec((1,) + shape, lambda b, t: (layer,) + (0,) * len(shape))
    return pl.pallas_call(
        functools.partial(_s5_kernel, ts=ts),
        out_shape=jax.ShapeDtypeStruct((bsz, s, S5_WIDTH), BF16),
        grid=(bsz, s // ts),
        in_specs=[
            pl.BlockSpec((1, ts, S5_WIDTH), lambda b, t: (b, t, 1)),
            const((16, 256, 256)), const((16, 256, 256)),
            const((_S5_SEG, 2 * S5_LANES)), const((_S5_SEG, 8 * S5_LANES)), const((sl, 2 * S5_LANES)),
            const((1, S5_WIDTH)), const((S5_WIDTH, S5_WIDTH)), const((1, S5_WIDTH)),
        ],
        out_specs=pl.BlockSpec((1, ts, S5_WIDTH), lambda b, t: (b, t, 0)),
        scratch_shapes=[
            pltpu.VMEM((2 * _S5_COLS, _S5_SEG * (sl + 8), LANE), F32),
            pltpu.VMEM((2 * _S5_COLS, _S5_SEG, LANE), F32),
            pltpu.VMEM((ts, 2 * S5_LANES), BF16),
            pltpu.VMEM((_S5_SEG, 2 * S5_LANES), F32)],
        compiler_params=pltpu.CompilerParams(
            dimension_semantics=("arbitrary", "arbitrary"), vmem_limit_bytes=48 * 1024 * 1024),
        name="s5",
    )(qku, bt, ct, lam8, seg_tab, pw, d_skip, glu_w, glu_b)


_ATT_STRIP = 256
_ATT_AHEAD = 3
_ATT_ONES = 16


def _alibi_tables(tq, tk):
    parts, rem = [], math.log2(math.e)
    for _ in range(3):
        p = float(np.asarray(rem, dtype=BF16))
        parts.append(p)
        rem -= p
    qa = ((np.arange(tq) // 64) * 64).astype(np.float32)
    qb = (np.arange(tq) % 64).astype(np.float32)
    ka = ((np.arange(tk) // 64) * 64).astype(np.float32)
    kb = (np.arange(tk) % 64).astype(np.float32)
    qx = np.zeros((DIFF_HEADS, tq, LANE), np.float32)
    kx = np.zeros((DIFF_HEADS, tk, LANE), np.float32)
    cval = np.zeros((DIFF_HEADS, 8, LANE), np.float32)
    for h in range(DIFF_HEADS):
        slope = 2.0 ** (-8.0 * (h + 1) / DIFF_HEADS)
        for n, p in enumerate(parts):
            qx[h, :, n] = qx[h, :, 3 + n] = slope * p
            kx[h, :, n] = ka
            kx[h, :, 3 + n] = kb
            qx[h, :, 6 + n] = -qa
            qx[h, :, 9 + n] = -qb
            kx[h, :, 6 + n] = kx[h, :, 9 + n] = slope * p
        cval[h] = slope * math.log2(math.e)
    qxt = np.ascontiguousarray(qx.transpose(0, 2, 1))
    return jnp.asarray(qxt, BF16), jnp.asarray(kx, BF16), jnp.asarray(cval, F32)


def _attn_kernel(q_ref, k_ref, vt_ref, qx_ref, kx_ref, c_ref, lam_ref, sg_ref, o_ref,
                 qt_scr, m_scr, acc_scr, *, tq, tk, lambda_init):
    qi = pl.program_id(2)
    c11 = c_ref[0, 0:1, 0:1]
    qt = q_ref[0, 0]
    row = lax.broadcasted_iota(jnp.int32, (LANE, 1), 0)
    zero = jnp.zeros_like(qt)
    for half in range(2):
        qt_scr[half, 0:LANE, :] = jnp.where((row >= DIFF_DH) == (half == 1), qt, zero)
        qt_scr[half, LANE:, :] = qx_ref[0]
    kx = kx_ref[0]
    nstrip = tq // _ATT_STRIP
    kpq = tq // tk
    ones_rows = jnp.ones((_ATT_ONES, tk), BF16)

    m_scr[...] = jnp.full_like(m_scr, NEG_BIG)
    acc_scr[...] = jnp.zeros_like(acc_scr)

    def blocks(entries):
        loaded = []
        for kj, diag in entries:
            start = pl.multiple_of(kj * tk, tk)
            ke = jnp.concatenate([k_ref[0, pl.ds(start, tk), :], kx], axis=1)
            vt = jnp.concatenate([vt_ref[0, kj], ones_rows], axis=0)
            shift = c11 * (kj * tk - qi * tq).astype(F32)
            loaded.append((ke, vt, shift))
        units = []
        for e, (kj, diag) in enumerate(entries):
            for half in range(2):
                for st in range(nstrip):
                    nk = tk if diag is None else min(tk, (st + 1) * _ATT_STRIP - diag * tk)
                    if nk > 0:
                        units.append((e, half, st, nk))

        def scores(unit):
            e, half, st, nk = unit
            return _dot(loaded[e][0][:nk], qt_scr[half, :, st * _ATT_STRIP:(st + 1) * _ATT_STRIP])

        pending = [scores(u) for u in units[:_ATT_AHEAD]]
        for n, (e, half, st, nk) in enumerate(units):
            s_t = pending.pop(0)
            if n + _ATT_AHEAD < len(units):
                pending.append(scores(units[n + _ATT_AHEAD]))
            _, vt, shift = loaded[e]
            diag = entries[e][1]
            cols = slice(st * _ATT_STRIP, (st + 1) * _ATT_STRIP)
            if diag is not None and diag * tk + nk - 1 > st * _ATT_STRIP:
                jrow = lax.broadcasted_iota(jnp.int32, (nk, _ATT_STRIP), 0)
                icol = lax.broadcasted_iota(jnp.int32, (nk, _ATT_STRIP), 1)
                s_t = jnp.where(jrow + diag * tk <= icol + st * _ATT_STRIP, s_t, NEG_BIG)
            m_old = m_scr[half, :, cols] - shift
            m_new = jnp.maximum(m_old, jnp.max(s_t, axis=0, keepdims=True))
            alpha = jnp.exp2(m_old - m_new)
            p = jnp.exp2(s_t - m_new)
            acc_scr[half, :, cols] = alpha * acc_scr[half, :, cols] + _dot(vt[:, :nk], p.astype(BF16))
            m_scr[half, :, cols] = m_new + shift

    def body(i, carry):
        blocks([(i * kpq + d, None) for d in range(kpq)])
        return carry

    lax.fori_loop(0, qi, body, 0)
    blocks([(qi * kpq + d, d) for d in range(kpq)])

    lam = (jnp.exp(jnp.sum(lam_ref[0, 0:1, :] * lam_ref[0, 1:2, :], axis=-1, keepdims=True))
           - jnp.exp(jnp.sum(lam_ref[0, 2:3, :] * lam_ref[0, 3:4, :], axis=-1, keepdims=True)) + lambda_init)
    norm = [acc_scr[half, 0:DIFF_DV, :] / acc_scr[half, DIFF_DV:DIFF_DV + 1, :] for half in range(2)]
    o_t = norm[0] - lam * norm[1]
    o = o_t.T
    ms = jnp.mean(o * o, axis=-1, keepdims=True)
    o = o * lax.rsqrt(ms + EPS) * sg_ref[0] * (1.0 - lambda_init)
    o_ref[0] = o.astype(o_ref.dtype)


def _diff_attention(ku, qt, vt, layer, lam_rows, subln_g, lambda_init):
    bsz, s, _ = ku.shape
    tk = vt.shape[3]
    tq = qt.shape[3]
    qx, kx, cval = _alibi_tables(tq, tk)
    return pl.pallas_call(
        functools.partial(_attn_kernel, tq=tq, tk=tk, lambda_init=lambda_init),
        out_shape=jax.ShapeDtypeStruct((bsz, s, DIFF_HEADS * DIFF_DV), BF16),
        grid=(bsz, DIFF_HEADS, s // tq),
        in_specs=[
            pl.BlockSpec((1, 1, LANE, tq), lambda b, h, i: (b, i, h, 0)),
            pl.BlockSpec((1, s, LANE), lambda b, h, i: (b, 0, h)),
            pl.BlockSpec((1, s // tk, DIFF_DV, tk), lambda b, h, i: (b, 0, h, 0)),
            pl.BlockSpec((1, LANE, tq), lambda b, h, i: (h, 0, 0)),
            pl.BlockSpec((1, tk, LANE), lambda b, h, i: (h, 0, 0)),
            pl.BlockSpec((1, 8, LANE), lambda b, h, i: (h, 0, 0)),
            pl.BlockSpec((1, 8, LANE), lambda b, h, i: (layer, 0, 0)),
            pl.BlockSpec((1, 1, LANE), lambda b, h, i: (layer, 0, 0)),
        ],
        out_specs=pl.BlockSpec((1, tq, LANE), lambda b, h, i: (b, i, h)),
        scratch_shapes=[
            pltpu.VMEM((2, 2 * LANE, tq), BF16), pltpu.VMEM((2, 1, tq), F32),
            pltpu.VMEM((2, DIFF_DV + _ATT_ONES, tq), F32)],
        compiler_params=pltpu.CompilerParams(
            dimension_semantics=("arbitrary", "arbitrary", "arbitrary"),
            vmem_limit_bytes=48 * 1024 * 1024),
        name="diff_attn",
    )(qt, ku, vt, qx, kx, cval, lam_rows, subln_g)


def _merge_kernel(x_ref, mod_ref, g_ref, wg_ref, oa_ref, ob_ref, oc_ref, wa_ref, wb_ref, wc_ref,
                  wo_ref, o_ref):
    b = pl.program_id(0)
    x = x_ref[0]
    h = _norm_mod(x, g_ref[0], _mod_rows(mod_ref, b, 1), _mod_rows(mod_ref, b, 0)).astype(BF16)
    merged = None
    for n, (ob, wb) in enumerate(((oa_ref, wa_ref), (ob_ref, wb_ref), (oc_ref, wc_ref))):
        gate = jax.nn.sigmoid(_dot(h, wg_ref[0, :, n * D_MODEL:(n + 1) * D_MODEL]))
        term = gate * _dot(ob[0], wb[0])
        merged = term if merged is None else merged + term
    y = _dot(merged.astype(BF16), wo_ref[0])
    o_ref[0] = x + _mod_rows(mod_ref, b, 2) * y


def _merge(x, mod, layer, g, w_all, o_a, o_b, o_c, w_a, w_b, w_c, w_out):
    bsz, s, d = x.shape
    tm = min(TM_MERGE, s)
    lay = lambda shape: pl.BlockSpec((1,) + shape, lambda b, i: (layer,) + (0,) * len(shape))
    tok = lambda n: pl.BlockSpec((1, tm, n), lambda b, i: (b, i, 0))
    return pl.pallas_call(
        _merge_kernel,
        out_shape=jax.ShapeDtypeStruct((bsz, s, d), F32),
        grid=(bsz, s // tm),
        in_specs=[
            tok(d),
            lay((8, mod.shape[2])),
            lay((1, d)), lay((d, _W_GATES)),
            tok(GLA_V), tok(S5_WIDTH), tok(DIFF_HEADS * DIFF_DV),
            lay((GLA_V, d)), lay((S5_WIDTH, d)), lay((DIFF_HEADS * DIFF_DV, d)), lay((d, d)),
        ],
        out_specs=tok(d),
        compiler_params=pltpu.CompilerParams(
            dimension_semantics=("arbitrary", "arbitrary"), vmem_limit_bytes=56 * 1024 * 1024),
        name="merge",
    )(x, mod, g, w_all, o_a, o_b, o_c, w_a, w_b, w_c, w_out)


_HALO = 16


def _ffn_kernel(x_ref, mod_ref, g_ref, wup_ref, cw_ref, cb_ref, wd_ref,
                o_ref, h_scr, halo_scr, up_scr, act_scr, *, tm, tf, f):
    b = pl.program_id(0)

    @pl.when(pl.program_id(1) == 0)
    def _():
        halo_scr[...] = jnp.zeros_like(halo_scr)

    x = x_ref[0]
    h = _norm_mod(x, g_ref[0], _mod_rows(mod_ref, b, 4), _mod_rows(mod_ref, b, 3)).astype(BF16)
    h_scr[0:_HALO, :] = halo_scr[...]
    h_scr[_HALO:, :] = h
    halo_scr[...] = h[tm - _HALO:, :]
    hh = h_scr[...]

    def conv(part, col):
        cw = cw_ref[0, :, col:col + tf]
        taps = [cw[n:n + 1, :] * up_scr[part, pl.ds(_HALO - 2 + n, tm), :] for n in range(3)]
        return taps[0] + taps[1] + taps[2] + cb_ref[0, :, col:col + tf]

    for c in range(f // tf):
        for part in range(2):
            col = part * f + c * tf
            up_scr[part] = _dot(hh, wup_ref[0, :, col:col + tf])
        a = conv(0, c * tf)
        gg = conv(1, f + c * tf)
        act_scr[:, c * tf:(c + 1) * tf] = (a * jax.nn.sigmoid(a) * gg).astype(BF16)

    o_ref[0] = x + _mod_rows(mod_ref, b, 5) * _dot(act_scr[...], wd_ref[0])


def _ffn(x, mod, layer, g, w_up, conv_w, conv_b, w_down):
    bsz, s, d = x.shape
    f = w_down.shape[1]
    tm = min(TM_FFN, s)
    tf = TF
    resident = lambda shape: pl.BlockSpec((1,) + shape, lambda b, i: (layer,) + (0,) * len(shape),
                                          pipeline_mode=pl.Buffered(1))
    return pl.pallas_call(
        functools.partial(_ffn_kernel, tm=tm, tf=tf, f=f),
        out_shape=jax.ShapeDtypeStruct((bsz, s, d), F32),
        grid=(bsz, s // tm),
        in_specs=[
            pl.BlockSpec((1, tm, d), lambda b, i: (b, i, 0)),
            pl.BlockSpec((1, 8, mod.shape[2]), lambda b, i: (layer, 0, 0)),
            resident((1, d)), resident((d, 2 * f)), resident((3, 2 * f)), resident((1, 2 * f)),
            resident((f, d)),
        ],
        out_specs=pl.BlockSpec((1, tm, d), lambda b, i: (b, i, 0)),
        scratch_shapes=[
            pltpu.VMEM((tm + _HALO, d), BF16), pltpu.VMEM((_HALO, d), BF16),
            pltpu.VMEM((2, tm + _HALO, tf), F32), pltpu.VMEM((tm, f), BF16)],
        compiler_params=pltpu.CompilerParams(
            dimension_semantics=("arbitrary", "arbitrary"), vmem_limit_bytes=56 * 1024 * 1024),
        name="ffn",
    )(x, mod, g, w_up, conv_w, conv_b, w_down)


def kernel(x, c, ada_w, ada_b, norm1_g, w_in, gla_gk_w, gla_gk_b, gla_norm_g, s5_lambda_re, s5_lambda_im, s5_b_re, s5_b_im, s5_c_re, s5_c_im, s5_d, s5_log_dt, s5_glu_w, s5_glu_b, diff_q_norm_g, diff_k_norm_g, diff_lambda_q1, diff_lambda_k1, diff_lambda_q2, diff_lambda_k2, diff_subln_g, w_branch_gla, w_branch_s5, w_branch_diff, w_out, norm2_g, ffn_w_up, ffn_conv_w, ffn_conv_b, ffn_w_down):
    depth, d = norm1_g.shape
    mod = _modulation(c, ada_w, ada_b)

    off_dv = _OFF_DQ + 2 * DIFF_HEADS * 2 * DIFF_DH
    w_bf = w_in.astype(BF16)
    w_all = jnp.concatenate(
        [w_bf[:, :, _OFF_GATES:], w_bf[:, :, _OFF_GLA:_OFF_GLR], w_bf[:, :, _OFF_SU:_OFF_GATES]],
        axis=2)
    w_lr = jnp.pad(w_in[:, :, _OFF_GLR:_OFF_SU], ((0, 0), (0, 0), (0, LANE - GLA_LR))).astype(BF16)
    w_qvt = jnp.swapaxes(jnp.concatenate(
        [w_in[:, :, _OFF_DQ:_OFF_DQ + _W_TILE], w_in[:, :, off_dv:_OFF_GATES]], axis=2), 1, 2).astype(BF16)
    w_a, w_b, w_c = (w.astype(BF16) for w in (w_branch_gla, w_branch_s5, w_branch_diff))
    w_o = w_out.astype(BF16)
    w_up, w_dn = ffn_w_up.astype(BF16), ffn_w_down.astype(BF16)
    glu_w = s5_glu_w.astype(BF16)
    g1, g2 = norm1_g.reshape(depth, 1, d), norm2_g.reshape(depth, 1, d)
    kg = jnp.tile(diff_q_norm_g * diff_k_norm_g, (1, 2)).reshape(depth, 1, LANE)
    sg = diff_subln_g.reshape(depth, 1, DIFF_DV)
    lam_rows = jnp.pad(
        jnp.stack([diff_lambda_q1, diff_lambda_k1, diff_lambda_q2, diff_lambda_k2], axis=1),
        ((0, 0), (0, 4), (0, LANE - DIFF_DH)))
    seq = x.shape[1]
    s5_tabs = _s5_tables(s5_lambda_re, s5_lambda_im, s5_b_re, s5_b_im, s5_c_re, s5_c_im, s5_log_dt,
                         min(TS, seq) // _S5_SEG)
    s5_d3 = s5_d.reshape(depth, 1, S5_WIDTH)
    glu_b3 = s5_glu_b.reshape(depth, 1, S5_WIDTH)
    conv_b3 = ffn_conv_b.reshape(depth, 1, -1)

    for l in range(depth):
        lambda_init = 0.8 - 0.6 * math.exp(-0.3 * l)
        p_gla = _inproj_gla(x, mod, l, g1, w_all, w_lr)
        ku, qt, vt = _inproj_qkuv(x, mod, l, g1, w_all, w_qvt, kg)
        o_a = _gla(p_gla, gla_gk_w[l], gla_gk_b[l], gla_norm_g[l])
        o_b = _s5(ku, l, s5_tabs, s5_d3, glu_w, glu_b3)
        o_c = _diff_attention(ku, qt, vt, l, lam_rows, sg, lambda_init)
        x = _merge(x, mod, l, g1, w_all, o_a, o_b, o_c, w_a, w_b, w_c, w_o)
        x = _ffn(x, mod, l, g2, w_up, ffn_conv_w, conv_b3, w_dn)
    return x
```

```python
import functools
import math

import jax
import jax.numpy as jnp
import numpy as np
from jax import lax
from jax.experimental import pallas as pl
from jax.experimental.pallas import tpu as pltpu

F32 = jnp.float32
BF16 = jnp.bfloat16

D_MODEL = 1024
GLA_HEADS = 4
GLA_DK = 64
GLA_DV = 128
GLA_LR = 16
GLA_TAU = 16.0
GLA_CHUNK = 64
GLA_QK = GLA_HEADS * GLA_DK
GLA_V = GLA_HEADS * GLA_DV
GLA_COLS = 2 * GLA_QK + 2 * GLA_V + 128
S5_WIDTH = 512
S5_GROUP = 16
S5_GROUPS = 32
S5_STATE = 64
S5_LANES = S5_GROUPS * S5_STATE
DIFF_HEADS = 4
DIFF_DH = 64
DIFF_DV = 128
D_FF = 2816
EPS = 1e-6
LANE = 128
NEG_BIG = -1e30

TM_IN = 1024
TG = 256
TS = 512
TQ = 1024
TK = 512
TM_MERGE = 512
TM_FFN = 512
TF = 256

_OFF_GLA = 0
_OFF_GLR = 1536
_OFF_SU = 1552
_OFF_DQ = 2064
_OFF_GATES = 3600
_IN_COLS = 6672

_W_GATES = 3 * D_MODEL
_W_GLA = 2 * GLA_QK + 2 * GLA_V
_W_GLA_BLK = _W_GATES // _W_GLA
_W_TILE = 512
_W_U_BLK = (_W_GATES + _W_GLA) // _W_TILE


def _dot(a, b):
    return jnp.dot(a, b, preferred_element_type=F32)


def _dot_nt(a, b):
    return lax.dot_general(a, b, (((1,), (1,)), ((), ())), preferred_element_type=F32)


def _dot_tn(a, b):
    return lax.dot_general(a, b, (((0,), (0,)), ((), ())), preferred_element_type=F32)


def _norm_mod(x, g, sc, sh):
    ms = jnp.mean(x * x, axis=-1, keepdims=True)
    y = x * lax.rsqrt(ms + EPS) * g
    return y * (1.0 + sc) + sh


def _mod_rows(mod_ref, b, k):
    return mod_ref[0, pl.ds(b, 1), k * D_MODEL:(k + 1) * D_MODEL]


def _mod_kernel(c_ref, w_ref, b_ref, o_ref):
    c = c_ref[...]
    ca = (c * jax.nn.sigmoid(c)).astype(BF16)
    o_ref[0] = _dot(ca, w_ref[0].astype(BF16)) + b_ref[0]


def _modulation(c, ada_w, ada_b):
    depth, d, n = ada_w.shape
    bsz = c.shape[0]
    cp = jnp.zeros((8, d), F32).at[:bsz].set(c)
    tn = 1536
    return pl.pallas_call(
        _mod_kernel,
        out_shape=jax.ShapeDtypeStruct((depth, 8, n), F32),
        grid=(depth, n // tn),
        in_specs=[
            pl.BlockSpec((8, d), lambda l, j: (0, 0)),
            pl.BlockSpec((1, d, tn), lambda l, j: (l, 0, j)),
            pl.BlockSpec((1, 1, tn), lambda l, j: (l, 0, j)),
        ],
        out_specs=pl.BlockSpec((1, 8, tn), lambda l, j: (l, 0, j)),
        name="adaln_mod",
    )(cp, ada_w, ada_b.reshape(depth, 1, n))


def _seg_rmsnorm(a, g128, scale):
    lane = lax.broadcasted_iota(jnp.int32, (1, LANE), 1)
    lo = (lane < DIFF_DH).astype(F32)
    hi = 1.0 - lo
    outs = []
    for hb in range(a.shape[-1] // LANE):
        xh = a[:, hb * LANE:(hb + 1) * LANE]
        sq = xh * xh
        s_lo = jnp.sum(sq * lo, axis=-1, keepdims=True)
        s_hi = jnp.sum(sq * hi, axis=-1, keepdims=True)
        ms = (s_lo * lo + s_hi * hi) * (1.0 / DIFF_DH)
        outs.append(xh * lax.rsqrt(ms + EPS) * g128 * scale)
    return jnp.concatenate(outs, axis=-1)


def _inproj_gla_kernel(x_ref, mod_ref, g_ref, w_ref, wlr_ref, o_ref):
    b = pl.program_id(0)
    h = _norm_mod(x_ref[0], g_ref[0], _mod_rows(mod_ref, b, 1), _mod_rows(mod_ref, b, 0)).astype(BF16)
    o_ref[0, :, 0:_W_GLA] = _dot(h, w_ref[0])
    o_ref[0, :, _W_GLA:] = _dot(h, wlr_ref[0])


def _inproj_gla(x, mod, layer, g, w_all, w_lr):
    bsz, s, d = x.shape
    tm = min(TM_IN, s)
    return pl.pallas_call(
        _inproj_gla_kernel,
        out_shape=jax.ShapeDtypeStruct((bsz, s, GLA_COLS), F32),
        grid=(bsz, s // tm),
        in_specs=[
            pl.BlockSpec((1, tm, d), lambda b, i: (b, i, 0)),
            pl.BlockSpec((1, 8, mod.shape[2]), lambda b, i: (layer, 0, 0)),
            pl.BlockSpec((1, 1, d), lambda b, i: (layer, 0, 0)),
            pl.BlockSpec((1, d, _W_GLA), lambda b, i: (layer, 0, _W_GLA_BLK)),
            pl.BlockSpec((1, d, LANE), lambda b, i: (layer, 0, 0)),
        ],
        out_specs=pl.BlockSpec((1, tm, GLA_COLS), lambda b, i: (b, i, 0)),
        compiler_params=pltpu.CompilerParams(
            dimension_semantics=("arbitrary", "arbitrary"), vmem_limit_bytes=48 * 1024 * 1024),
        name="inproj_gla",
    )(x, mod, g, w_all, w_lr)


def _inproj_qkuv_kernel(x_ref, mod_ref, g_ref, wu_ref, wk_ref, wqvt_ref, kg_ref, o_ref, qt_ref, vt_ref):
    b = pl.program_id(0)
    h = _norm_mod(x_ref[0], g_ref[0], _mod_rows(mod_ref, b, 1), _mod_rows(mod_ref, b, 0)).astype(BF16)
    tn = _W_TILE
    o_ref[0, :, 0:tn] = _seg_rmsnorm(_dot(h, wk_ref[0]), kg_ref[0], 1.0).astype(o_ref.dtype)
    o_ref[0, :, tn:] = _dot(h, wu_ref[0]).astype(o_ref.dtype)
    qvt = _dot_nt(wqvt_ref[0], h)
    qscale = DIFF_DH ** -0.5 * math.log2(math.e)
    for seg in range(tn // DIFF_DH):
        rows = slice(seg * DIFF_DH, (seg + 1) * DIFF_DH)
        qs = qvt[rows]
        ms = jnp.mean(qs * qs, axis=0, keepdims=True)
        qt_ref[0, 0, rows, :] = (qs * (lax.rsqrt(ms + EPS) * qscale)).astype(qt_ref.dtype)
    vt = qvt[tn:].astype(vt_ref.dtype)
    tk = vt_ref.shape[3]
    for n in range(vt_ref.shape[1]):
        vt_ref[0, n] = vt[:, n * tk:(n + 1) * tk]


def _inproj_qkuv(x, mod, layer, g, w_all, w_qvt, k_gain):
    bsz, s, d = x.shape
    tm = min(TQ, s)
    tk = min(TK, s)
    tn = _W_TILE
    resident = lambda shape, *idx: pl.BlockSpec((1,) + shape, lambda b, i: (layer,) + idx,
                                                pipeline_mode=pl.Buffered(1))
    return pl.pallas_call(
        _inproj_qkuv_kernel,
        out_shape=(jax.ShapeDtypeStruct((bsz, s, 2 * tn), BF16),
                   jax.ShapeDtypeStruct((bsz, s // tm, tn, tm), BF16),
                   jax.ShapeDtypeStruct((bsz, s // tk, tn, tk), BF16)),
        grid=(bsz, s // tm),
        in_specs=[
            pl.BlockSpec((1, tm, d), lambda b, i: (b, i, 0)),
            pl.BlockSpec((1, 8, mod.shape[2]), lambda b, i: (layer, 0, 0)),
            resident((1, d), 0, 0),
            resident((d, tn), 0, _W_U_BLK), resident((d, tn), 0, _W_U_BLK + 2),
            resident((2 * tn, d), 0, 0), resident((1, LANE), 0, 0),
        ],
        out_specs=(
            pl.BlockSpec((1, tm, 2 * tn), lambda b, i: (b, i, 0)),
            pl.BlockSpec((1, 1, tn, tm), lambda b, i: (b, i, 0, 0)),
            pl.BlockSpec((1, tm // tk, tn, tk), lambda b, i: (b, i, 0, 0)),
        ),
        compiler_params=pltpu.CompilerParams(
            dimension_semantics=("arbitrary", "arbitrary"), vmem_limit_bytes=48 * 1024 * 1024),
        name="inproj_qkuv",
    )(x, mod, g, w_all, w_all, w_qvt, k_gain)


def _gla_kernel(p_ref, gkw_ref, gkb_ref, ng_ref, ltri_ref, lall_ref, o_ref, s_scr, *, nchunk):
    @pl.when(pl.program_id(1) == 0)
    def _():
        s_scr[...] = jnp.zeros_like(s_scr)

    blk = p_ref[0]
    q = blk[:, 0:GLA_QK]
    k = blk[:, GLA_QK:2 * GLA_QK]
    v = blk[:, 2 * GLA_QK:2 * GLA_QK + GLA_V]
    r = blk[:, 2 * GLA_QK + GLA_V:2 * GLA_QK + 2 * GLA_V]
    lr = blk[:, 2 * GLA_QK + 2 * GLA_V:]
    z = _dot(lr.astype(BF16), gkw_ref[...]) + gkb_ref[...]
    la = (jnp.minimum(z, 0.0) - jnp.log1p(jnp.exp(-jnp.abs(z)))) * (1.0 / GLA_TAU)
    la_hi = la.astype(BF16)
    la_lo = (la - la_hi.astype(F32)).astype(BF16)
    ltri = ltri_ref[...]
    lall = lall_ref[...]
    bcum = _dot(ltri, la_hi) + _dot(ltri, la_lo)
    btot = _dot(lall, la_hi) + _dot(lall, la_lo)
    qd = q * (GLA_DK ** -0.5) * jnp.exp(bcum)
    kd = (k * jnp.exp(-bcum)).astype(BF16)
    ke = (k * jnp.exp(btot - bcum)).astype(BF16)
    vb = v.astype(BF16)
    gate = r * jax.nn.sigmoid(r)

    lane_head = lax.broadcasted_iota(jnp.int32, (1, GLA_QK), 1) // GLA_DK
    rowi = lax.broadcasted_iota(jnp.int32, (GLA_QK, GLA_CHUNK), 0) % GLA_CHUNK
    colj = lax.broadcasted_iota(jnp.int32, (GLA_QK, GLA_CHUNK), 1)
    tril = rowi >= colj
    ones_cv = jnp.ones((GLA_CHUNK, GLA_DV), BF16)
    ng = ng_ref[...]

    chunks = [slice(c * GLA_CHUNK, (c + 1) * GLA_CHUNK) for c in range(nchunk)]

    states = []
    state = s_scr[...]
    for rows in chunks:
        states.append(state.astype(BF16))
        kv = _dot_tn(ke[rows], vb[rows])
        kvd = jnp.concatenate(
            [kv[h * GLA_DK:(h + 1) * GLA_DK, h * GLA_DV:(h + 1) * GLA_DV] for h in range(GLA_HEADS)],
            axis=0)
        ldec = _dot_tn(la_hi[rows], ones_cv) + _dot_tn(la_lo[rows], ones_cv)
        state = jnp.exp(ldec) * state + kvd
    s_scr[...] = state

    qms = [jnp.concatenate(
        [jnp.where(lane_head == h, qd[rows], 0.0).astype(BF16) for h in range(GLA_HEADS)], axis=0)
        for rows in chunks]
    atts = [_dot_nt(qm, kd[rows]) for qm, rows in zip(qms, chunks)]
    inters = [_dot(qm, st) for qm, st in zip(qms, states)]
    atts = [jnp.where(tril, a, 0.0).astype(BF16) for a in atts]
    heads = [slice(h * GLA_CHUNK, (h + 1) * GLA_CHUNK) for h in range(GLA_HEADS)]
    outs = [[_dot(att[hr], vb[rows][:, h * GLA_DV:(h + 1) * GLA_DV]) + inter[hr]
             for h, hr in enumerate(heads)]
            for att, inter, rows in zip(atts, inters, chunks)]
    for o_heads, rows in zip(outs, chunks):
        normed = [o_h * lax.rsqrt(jnp.mean(o_h * o_h, axis=-1, keepdims=True) + EPS) * ng for o_h in o_heads]
        o = jnp.concatenate(normed, axis=1) * gate[rows]
        o_ref[0, rows, :] = o.astype(o_ref.dtype)


def _gla(p, gk_w, gk_b, norm_g):
    bsz, s, _ = p.shape
    tg = min(TG, s)
    nchunk = tg // GLA_CHUNK
    gkw = jnp.zeros((LANE, GLA_QK), F32).at[:GLA_LR].set(gk_w).astype(BF16)
    ri = jnp.arange(tg)[:, None]
    ci = jnp.arange(tg)[None, :]
    same = (ri // GLA_CHUNK) == (ci // GLA_CHUNK)
    ltri = (same & (ri >= ci)).astype(BF16)
    lall = same.astype(BF16)
    const = lambda shape: pl.BlockSpec(shape, lambda b, t: (0,) * len(shape))
    return pl.pallas_call(
        functools.partial(_gla_kernel, nchunk=nchunk),
        out_shape=jax.ShapeDtypeStruct((bsz, s, GLA_V), BF16),
        grid=(bsz, s // tg),
        in_specs=[
            pl.BlockSpec((1, tg, GLA_COLS), lambda b, t: (b, t, 0)),
            const((LANE, GLA_QK)), const((1, GLA_QK)), const((1, GLA_DV)),
            const((tg, tg)), const((tg, tg)),
        ],
        out_specs=pl.BlockSpec((1, tg, GLA_V), lambda b, t: (b, t, 0)),
        scratch_shapes=[pltpu.VMEM((GLA_QK, GLA_DV), F32)],
        compiler_params=pltpu.CompilerParams(dimension_semantics=("arbitrary", "arbitrary")),
        name="gla",
    )(p, gkw, gk_b.reshape(1, GLA_QK), norm_g.reshape(1, GLA_DV), ltri, lall)


_S5_COLS = S5_LANES // LANE
_S5_CG = 8
_S5_SEG = 8


def _s5_tables(lam_re, lam_im, b_re, b_im, c_re, c_im, log_dt, sl):
    nl = lam_re.shape[0]
    dt = jnp.exp(log_dt.astype(F32))[..., None]
    ar = (lam_re.astype(F32) * dt).reshape(nl, 1, S5_LANES)
    ai = (lam_im.astype(F32) * dt).reshape(nl, 1, S5_LANES)

    def powers(ks):
        kk = jnp.asarray(ks, F32)[None, :, None]
        mag = jnp.exp(kk * ar)
        return mag * jnp.cos(kk * ai), mag * jnp.sin(kk * ai)

    lr1, li1 = powers([1.0])
    lbr, lbi = lr1.reshape(lam_re.shape), li1.reshape(lam_re.shape)
    den = lam_re * lam_re + lam_im * lam_im
    nr = ((lbr - 1.0) * lam_re + lbi * lam_im) / den
    ni = (lbi * lam_re - (lbr - 1.0) * lam_im) / den
    bbr = nr[..., None] * b_re - ni[..., None] * b_im
    bbi = nr[..., None] * b_im + ni[..., None] * b_re
    eye4 = jnp.eye(4, dtype=F32)
    band = eye4[np.arange(8) % 4]

    def b_tiles(bb):
        bb4 = bb.reshape(nl, 8, 4, S5_STATE, S5_GROUP)
        small = jnp.einsum('lnkph,kj->lnkhjp', bb4, eye4).reshape(nl, 8, 64, 256)
        return jnp.einsum('lnrc,nq->lnqrc', small, band).reshape(nl, 8, 256, 256)

    def c_tiles(cc):
        cc4 = cc.reshape(nl, 8, 4, S5_GROUP, S5_STATE)
        small = jnp.einsum('lnkhp,kj->lnkpjh', cc4, eye4).reshape(nl, 8, 256, 64)
        return jnp.einsum('lnrc,nq->lnrqc', small, band).reshape(nl, 8, 256, 256)

    bt = jnp.concatenate([b_tiles(bbr), b_tiles(bbi)], axis=1).astype(BF16)
    ct = jnp.concatenate([c_tiles(c_re), c_tiles(-c_im)], axis=1).astype(BF16)
    lam8 = jnp.concatenate([jnp.broadcast_to(lr1, (nl, _S5_SEG, S5_LANES)),
                            jnp.broadcast_to(li1, (nl, _S5_SEG, S5_LANES))], axis=2)
    dsteps = np.array([1, 2, 4])
    keep = jnp.asarray(np.arange(_S5_SEG)[None, :] >= dsteps[:, None], F32)
    pr3, pi3 = powers(dsteps * float(sl))
    logstep = jnp.stack([keep[None, :, :, None] * pr3[:, :, None, :],
                         keep[None, :, :, None] * pi3[:, :, None, :]], axis=2)
    logstep = logstep.transpose(0, 3, 1, 2, 4).reshape(nl, _S5_SEG, 6 * S5_LANES)
    qr, qi = powers(np.arange(1, _S5_SEG + 1) * float(sl))
    seg_tab = jnp.concatenate([logstep, qr, qi], axis=2)
    pwr, pwi = powers(np.arange(1, sl + 1))
    pw = jnp.concatenate([pwr, pwi], axis=2)
    return bt, ct, lam8, seg_tab, pw


def _s5_kernel(u_ref, bt_ref, ct_ref, lam_ref, seg_ref, pw_ref, d_ref, gw_ref, gb_ref, o_ref,
               bu_scr, c_scr, x_scr, car_scr, *, ts):
    sl = ts // _S5_SEG
    pitch = sl + 8

    @pl.when(pl.program_id(1) == 0)
    def _():
        car_scr[...] = jnp.zeros_like(car_scr)

    u = u_ref[0]
    for t in range(16):
        half = (t % 8) // 4
        res = _dot(u[:, half * 256:(half + 1) * 256], bt_ref[0, t])
        for s in range(_S5_SEG):
            for w in range(2):
                bu_scr[2 * t + w, s * pitch:s * pitch + sl, :] = res[s * sl:(s + 1) * sl, w * LANE:(w + 1) * LANE]

    def lanes(ref, idx, j):
        return ref[0, :, idx * S5_LANES + j * LANE: idx * S5_LANES + (j + 1) * LANE]

    for j0 in range(0, _S5_COLS, _S5_CG):
        cols = list(range(j0, j0 + _S5_CG))

        def body(t, carry, cols=cols):
            new = [None] * (2 * len(cols))
            for n, j in enumerate(cols):
                xr, xi = carry[n], carry[len(cols) + n]
                lr_, li_ = lanes(lam_ref, 0, j), lanes(lam_ref, 1, j)
                vr = bu_scr[j, pl.ds(t, _S5_SEG, stride=pitch), :]
                vi = bu_scr[_S5_COLS + j, pl.ds(t, _S5_SEG, stride=pitch), :]
                nr = lr_ * xr - li_ * xi + vr
                ni = lr_ * xi + li_ * xr + vi
                bu_scr[j, pl.ds(t, _S5_SEG, stride=pitch), :] = nr
                bu_scr[_S5_COLS + j, pl.ds(t, _S5_SEG, stride=pitch), :] = ni
                new[n], new[len(cols) + n] = nr, ni
            return tuple(new)

        zero = jnp.zeros((_S5_SEG, LANE), F32)
        fin = lax.fori_loop(0, sl, body, (zero,) * (2 * len(cols)), unroll=4)

        row = lax.broadcasted_iota(jnp.int32, (_S5_SEG, LANE), 0)
        for n, j in enumerate(cols):
            gr, gi = fin[n], fin[len(cols) + n]
            for si, dstep in enumerate((1, 2, 4)):
                tr, ti = lanes(seg_ref, 2 * si, j), lanes(seg_ref, 2 * si + 1, j)
                sr = pltpu.roll(gr, dstep, 0)
                sm = pltpu.roll(gi, dstep, 0)
                gr, gi = gr + tr * sr - ti * sm, gi + tr * sm + ti * sr
            cr = car_scr[:, j * LANE:(j + 1) * LANE]
            ci = car_scr[:, S5_LANES + j * LANE:S5_LANES + (j + 1) * LANE]
            qr, qi = lanes(seg_ref, 6, j), lanes(seg_ref, 7, j)
            gr, gi = gr + qr * cr - qi * ci, gi + qr * ci + qi * cr
            c_scr[j] = jnp.where(row == 0, cr, pltpu.roll(gr, 1, 0))
            c_scr[_S5_COLS + j] = jnp.where(row == 0, ci, pltpu.roll(gi, 1, 0))
            car_scr[:, j * LANE:(j + 1) * LANE] = jnp.broadcast_to(gr[_S5_SEG - 1:, :], (_S5_SEG, LANE))
            car_scr[:, S5_LANES + j * LANE:S5_LANES + (j + 1) * LANE] = jnp.broadcast_to(
                gi[_S5_SEG - 1:, :], (_S5_SEG, LANE))

    for j in range(_S5_COLS):
        pr_ = pw_ref[0, :, j * LANE:(j + 1) * LANE]
        pi_ = pw_ref[0, :, S5_LANES + j * LANE:S5_LANES + (j + 1) * LANE]
        for s in range(_S5_SEG):
            cr = c_scr[j, s:s + 1, :]
            ci = c_scr[_S5_COLS + j, s:s + 1, :]
            xr = bu_scr[j, s * pitch:s * pitch + sl, :] + pr_ * cr - pi_ * ci
            xi = bu_scr[_S5_COLS + j, s * pitch:s * pitch + sl, :] + pr_ * ci + pi_ * cr
            x_scr[s * sl:(s + 1) * sl, j * LANE:(j + 1) * LANE] = xr.astype(BF16)
            x_scr[s * sl:(s + 1) * sl, S5_LANES + j * LANE:S5_LANES + (j + 1) * LANE] = xi.astype(BF16)

    ys = []
    for half in range(2):
        acc = None
        for t in [half * 4 + i for i in range(4)] + [8 + half * 4 + i for i in range(4)]:
            part = _dot(x_scr[:, t * 256:(t + 1) * 256], ct_ref[0, t])
            acc = part if acc is None else acc + part
        ys.append(acc)
    y = jnp.concatenate(ys, axis=1) + d_ref[0] * u.astype(F32)
    g = 0.5 * y * (1.0 + jnp.tanh(math.sqrt(2.0 / math.pi) * (y + 0.044715 * (y * y * y))))
    zz = _dot(g.astype(BF16), gw_ref[0]) + gb_ref[0]
    o_ref[0] = (g * jax.nn.sigmoid(zz)).astype(o_ref.dtype)


def _s5(qku, layer, tables, d_skip, glu_w, glu_b):
    bsz, s, _ = qku.shape
    ts = min(TS, s)
    sl = ts // _S5_SEG
    bt, ct, lam8, seg_tab, pw = tables
    const = lambda shape: pl.BlockSpec((1,) + shape, lambda b, t: (layer,) + (0,) * len(shape))
    return pl.pallas_call(
        functools.partial(_s5_kernel, ts=ts),
        out_shape=jax.ShapeDtypeStruct((bsz, s, S5_WIDTH), BF16),
        grid=(bsz, s // ts),
        in_specs=[
            pl.BlockSpec((1, ts, S5_WIDTH), lambda b, t: (b, t, 1)),
            const((16, 256, 256)), const((16, 256, 256)),
            const((_S5_SEG, 2 * S5_LANES)), const((_S5_SEG, 8 * S5_LANES)), const((sl, 2 * S5_LANES)),
            const((1, S5_WIDTH)), const((S5_WIDTH, S5_WIDTH)), const((1, S5_WIDTH)),
        ],
        out_specs=pl.BlockSpec((1, ts, S5_WIDTH), lambda b, t: (b, t, 0)),
        scratch_shapes=[
            pltpu.VMEM((2 * _S5_COLS, _S5_SEG * (sl + 8), LANE), F32),
            pltpu.VMEM((2 * _S5_COLS, _S5_SEG, LANE), F32),
            pltpu.VMEM((ts, 2 * S5_LANES), BF16),
            pltpu.VMEM((_S5_SEG, 2 * S5_LANES), F32)],
        compiler_params=pltpu.CompilerParams(
            dimension_semantics=("arbitrary", "arbitrary"), vmem_limit_bytes=48 * 1024 * 1024),
        name="s5",
    )(qku, bt, ct, lam8, seg_tab, pw, d_skip, glu_w, glu_b)


_ATT_STRIP = 256
_ATT_AHEAD = 3
_ATT_ONES = 16


def _alibi_tables(tq, tk):
    parts, rem = [], math.log2(math.e)
    for _ in range(3):
        p = float(np.asarray(rem, dtype=BF16))
        parts.append(p)
        rem -= p
    qa = ((np.arange(tq) // 64) * 64).astype(np.float32)
    qb = (np.arange(tq) % 64).astype(np.float32)
    ka = ((np.arange(tk) // 64) * 64).astype(np.float32)
    kb = (np.arange(tk) % 64).astype(np.float32)
    qx = np.zeros((DIFF_HEADS, tq, LANE), np.float32)
    kx = np.zeros((DIFF_HEADS, tk, LANE), np.float32)
    cval = np.zeros((DIFF_HEADS, 8, LANE), np.float32)
    for h in range(DIFF_HEADS):
        slope = 2.0 ** (-8.0 * (h + 1) / DIFF_HEADS)
        for n, p in enumerate(parts):
            qx[h, :, n] = qx[h, :, 3 + n] = slope * p
            kx[h, :, n] = ka
            kx[h, :, 3 + n] = kb
            qx[h, :, 6 + n] = -qa
            qx[h, :, 9 + n] = -qb
            kx[h, :, 6 + n] = kx[h, :, 9 + n] = slope * p
        cval[h] = slope * math.log2(math.e)
    qxt = np.ascontiguousarray(qx.transpose(0, 2, 1))
    return jnp.asarray(qxt, BF16), jnp.asarray(kx, BF16), jnp.asarray(cval, F32)


def _attn_kernel(lo_ref, q_ref, k_ref, vt_ref, qx_ref, kx_ref, c_ref, lam_ref, sg_ref, o_ref,
                 qt_scr, m_scr, acc_scr, *, tq, tk, lambda_init, layer):
    qi = pl.program_id(2)
    lo = lo_ref[layer, pl.program_id(1), qi]
    c11 = c_ref[0, 0:1, 0:1]
    qt = q_ref[0, 0]
    row = lax.broadcasted_iota(jnp.int32, (LANE, 1), 0)
    zero = jnp.zeros_like(qt)
    for half in range(2):
        qt_scr[half, 0:LANE, :] = jnp.where((row >= DIFF_DH) == (half == 1), qt, zero)
        qt_scr[half, LANE:, :] = qx_ref[0]
    kx = kx_ref[0]
    nstrip = tq // _ATT_STRIP
    kpq = tq // tk
    ones_rows = jnp.ones((_ATT_ONES, tk), BF16)

    m_scr[...] = jnp.full_like(m_scr, NEG_BIG)
    acc_scr[...] = jnp.zeros_like(acc_scr)

    def blocks(entries):
        loaded = []
        for kj, diag in entries:
            start = pl.multiple_of(kj * tk, tk)
            ke = jnp.concatenate([k_ref[0, pl.ds(start, tk), :], kx], axis=1)
            vt = jnp.concatenate([vt_ref[0, kj], ones_rows], axis=0)
            shift = c11 * (kj * tk - qi * tq).astype(F32)
            loaded.append((ke, vt, shift))
        units = []
        for e, (kj, diag) in enumerate(entries):
            for half in range(2):
                for st in range(nstrip):
                    nk = tk if diag is None else min(tk, (st + 1) * _ATT_STRIP - diag * tk)
                    if nk > 0:
                        units.append((e, half, st, nk))

        def scores(unit):
            e, half, st, nk = unit
            return _dot(loaded[e][0][:nk], qt_scr[half, :, st * _ATT_STRIP:(st + 1) * _ATT_STRIP])

        pending = [scores(u) for u in units[:_ATT_AHEAD]]
        for n, (e, half, st, nk) in enumerate(units):
            s_t = pending.pop(0)
            if n + _ATT_AHEAD < len(units):
                pending.append(scores(units[n + _ATT_AHEAD]))
            _, vt, shift = loaded[e]
            diag = entries[e][1]
            cols = slice(st * _ATT_STRIP, (st + 1) * _ATT_STRIP)
            if diag is not None and diag * tk + nk - 1 > st * _ATT_STRIP:
                jrow = lax.broadcasted_iota(jnp.int32, (nk, _ATT_STRIP), 0)
                icol = lax.broadcasted_iota(jnp.int32, (nk, _ATT_STRIP), 1)
                s_t = jnp.where(jrow + diag * tk <= icol + st * _ATT_STRIP, s_t, NEG_BIG)
            m_old = m_scr[half, :, cols] - shift
            m_new = jnp.maximum(m_old, jnp.max(s_t, axis=0, keepdims=True))
            alpha = jnp.exp2(m_old - m_new)
            p = jnp.exp2(s_t - m_new)
            acc_scr[half, :, cols] = alpha * acc_scr[half, :, cols] + _dot(vt[:, :nk], p.astype(BF16))
            m_scr[half, :, cols] = m_new + shift

    def body(i, carry):
        blocks([(i * kpq + d, None) for d in range(kpq)])
        return carry

    lax.fori_loop(lo, qi, body, 0)
    blocks([(qi * kpq + d, d) for d in range(kpq)])

    lam = (jnp.exp(jnp.sum(lam_ref[0, 0:1, :] * lam_ref[0, 1:2, :], axis=-1, keepdims=True))
           - jnp.exp(jnp.sum(lam_ref[0, 2:3, :] * lam_ref[0, 3:4, :], axis=-1, keepdims=True)) + lambda_init)
    norm = [acc_scr[half, 0:DIFF_DV, :] / acc_scr[half, DIFF_DV:DIFF_DV + 1, :] for half in range(2)]
    o_t = norm[0] - lam * norm[1]
    o = o_t.T
    ms = jnp.mean(o * o, axis=-1, keepdims=True)
    o = o * lax.rsqrt(ms + EPS) * sg_ref[0] * (1.0 - lambda_init)
    o_ref[0] = o.astype(o_ref.dtype)


_F32_EXP2_ZERO = 152.0


def _first_key_groups(qk_gain, seq, tq, tk):
    gmax = jnp.max(jnp.abs(qk_gain), axis=1)
    qnorm = 8.0 * DIFF_DH ** -0.5 * math.log2(math.e)
    bound = 1.02 * qnorm * 8.0 * gmax
    thresh = (2.0 * bound + _F32_EXP2_ZERO)[:, None, None]
    slopes = 2.0 ** (-8.0 * np.arange(1, DIFF_HEADS + 1) / DIFF_HEADS)
    cvals = jnp.asarray(slopes * math.log2(math.e), F32)[None, :, None]
    q0 = jnp.asarray(np.arange(seq // tq) * tq, F32)[None, None, :]
    kj_min = jnp.ceil((q0 + 1.0 - thresh / cvals) / tk - 1.0)
    kj_min = jnp.clip(kj_min, 0.0, float(seq // tk)).astype(jnp.int32)
    return kj_min // (tq // tk)


def _diff_attention(ku, qt, vt, layer, first_groups, lam_rows, subln_g, lambda_init):
    bsz, s, _ = ku.shape
    tk = vt.shape[3]
    tq = qt.shape[3]
    qx, kx, cval = _alibi_tables(tq, tk)
    return pl.pallas_call(
        functools.partial(_attn_kernel, tq=tq, tk=tk, lambda_init=lambda_init, layer=layer),
        out_shape=jax.ShapeDtypeStruct((bsz, s, DIFF_HEADS * DIFF_DV), BF16),
        grid=(bsz, DIFF_HEADS, s // tq),
        in_specs=[
            pl.BlockSpec(memory_space=pltpu.SMEM),
            pl.BlockSpec((1, 1, LANE, tq), lambda b, h, i: (b, i, h, 0)),
            pl.BlockSpec((1, s, LANE), lambda b, h, i: (b, 0, h)),
            pl.BlockSpec((1, s // tk, DIFF_DV, tk), lambda b, h, i: (b, 0, h, 0)),
            pl.BlockSpec((1, LANE, tq), lambda b, h, i: (h, 0, 0)),
            pl.BlockSpec((1, tk, LANE), lambda b, h, i: (h, 0, 0)),
            pl.BlockSpec((1, 8, LANE), lambda b, h, i: (h, 0, 0)),
            pl.BlockSpec((1, 8, LANE), lambda b, h, i: (layer, 0, 0)),
            pl.BlockSpec((1, 1, LANE), lambda b, h, i: (layer, 0, 0)),
        ],
        out_specs=pl.BlockSpec((1, tq, LANE), lambda b, h, i: (b, i, h)),
        scratch_shapes=[
            pltpu.VMEM((2, 2 * LANE, tq), BF16), pltpu.VMEM((2, 1, tq), F32),
            pltpu.VMEM((2, DIFF_DV + _ATT_ONES, tq), F32)],
        compiler_params=pltpu.CompilerParams(
            dimension_semantics=("arbitrary", "arbitrary", "arbitrary"),
            vmem_limit_bytes=48 * 1024 * 1024),
        name="diff_attn",
    )(first_groups, qt, ku, vt, qx, kx, cval, lam_rows, subln_g)


def _merge_kernel(x_ref, mod_ref, g_ref, wg_ref, oa_ref, ob_ref, oc_ref, wa_ref, wb_ref, wc_ref,
                  wo_ref, o_ref):
    b = pl.program_id(0)
    x = x_ref[0]
    h = _norm_mod(x, g_ref[0], _mod_rows(mod_ref, b, 1), _mod_rows(mod_ref, b, 0)).astype(BF16)
    merged = None
    for n, (ob, wb) in enumerate(((oa_ref, wa_ref), (ob_ref, wb_ref), (oc_ref, wc_ref))):
        gate = jax.nn.sigmoid(_dot(h, wg_ref[0, :, n * D_MODEL:(n + 1) * D_MODEL]))
        term = gate * _dot(ob[0], wb[0])
        merged = term if merged is None else merged + term
    y = _dot(merged.astype(BF16), wo_ref[0])
    o_ref[0] = x + _mod_rows(mod_ref, b, 2) * y


def _merge(x, mod, layer, g, w_all, o_a, o_b, o_c, w_a, w_b, w_c, w_out):
    bsz, s, d = x.shape
    tm = min(TM_MERGE, s)
    lay = lambda shape: pl.BlockSpec((1,) + shape, lambda b, i: (layer,) + (0,) * len(shape))
    tok = lambda n: pl.BlockSpec((1, tm, n), lambda b, i: (b, i, 0))
    return pl.pallas_call(
        _merge_kernel,
        out_shape=jax.ShapeDtypeStruct((bsz, s, d), F32),
        grid=(bsz, s // tm),
        in_specs=[
            tok(d),
            lay((8, mod.shape[2])),
            lay((1, d)), lay((d, _W_GATES)),
            tok(GLA_V), tok(S5_WIDTH), tok(DIFF_HEADS * DIFF_DV),
            lay((GLA_V, d)), lay((S5_WIDTH, d)), lay((DIFF_HEADS * DIFF_DV, d)), lay((d, d)),
        ],
        out_specs=tok(d),
        compiler_params=pltpu.CompilerParams(
            dimension_semantics=("arbitrary", "arbitrary"), vmem_limit_bytes=56 * 1024 * 1024),
        name="merge",
    )(x, mod, g, w_all, o_a, o_b, o_c, w_a, w_b, w_c, w_out)


_HALO = 16


def _ffn_kernel(x_ref, mod_ref, g_ref, wup_ref, cw_ref, cb_ref, wd_ref,
                o_ref, h_scr, halo_scr, up_scr, act_scr, *, tm, tf, f):
    b = pl.program_id(0)

    @pl.when(pl.program_id(1) == 0)
    def _():
        halo_scr[...] = jnp.zeros_like(halo_scr)

    x = x_ref[0]
    h = _norm_mod(x, g_ref[0], _mod_rows(mod_ref, b, 4), _mod_rows(mod_ref, b, 3)).astype(BF16)
    h_scr[0:_HALO, :] = halo_scr[...]
    h_scr[_HALO:, :] = h
    halo_scr[...] = h[tm - _HALO:, :]
    hh = h_scr[...]

    def conv(part, col):
        cw = cw_ref[0, :, col:col + tf]
        taps = [cw[n:n + 1, :] * up_scr[part, pl.ds(_HALO - 2 + n, tm), :] for n in range(3)]
        return taps[0] + taps[1] + taps[2] + cb_ref[0, :, col:col + tf]

    for c in range(f // tf):
        for part in range(2):
            col = part * f + c * tf
            up_scr[part] = _dot(hh, wup_ref[0, :, col:col + tf])
        a = conv(0, c * tf)
        gg = conv(1, f + c * tf)
        act_scr[:, c * tf:(c + 1) * tf] = (a * jax.nn.sigmoid(a) * gg).astype(BF16)

    o_ref[0] = x + _mod_rows(mod_ref, b, 5) * _dot(act_scr[...], wd_ref[0])


def _ffn(x, mod, layer, g, w_up, conv_w, conv_b, w_down):
    bsz, s, d = x.shape
    f = w_down.shape[1]
    tm = min(TM_FFN, s)
    tf = TF
    resident = lambda shape: pl.BlockSpec((1,) + shape, lambda b, i: (layer,) + (0,) * len(shape),
                                          pipeline_mode=pl.Buffered(1))
    return pl.pallas_call(
        functools.partial(_ffn_kernel, tm=tm, tf=tf, f=f),
        out_shape=jax.ShapeDtypeStruct((bsz, s, d), F32),
        grid=(bsz, s // tm),
        in_specs=[
            pl.BlockSpec((1, tm, d), lambda b, i: (b, i, 0)),
            pl.BlockSpec((1, 8, mod.shape[2]), lambda b, i: (layer, 0, 0)),
            resident((1, d)), resident((d, 2 * f)), resident((3, 2 * f)), resident((1, 2 * f)),
            resident((f, d)),
        ],
        out_specs=pl.BlockSpec((1, tm, d), lambda b, i: (b, i, 0)),
        scratch_shapes=[
            pltpu.VMEM((tm + _HALO, d), BF16), pltpu.VMEM((_HALO, d), BF16),
            pltpu.VMEM((2, tm + _HALO, tf), F32), pltpu.VMEM((tm, f), BF16)],
        compiler_params=pltpu.CompilerParams(
            dimension_semantics=("arbitrary", "arbitrary"), vmem_limit_bytes=56 * 1024 * 1024),
        name="ffn",
    )(x, mod, g, w_up, conv_w, conv_b, w_down)


def kernel(x, c, ada_w, ada_b, norm1_g, w_in, gla_gk_w, gla_gk_b, gla_norm_g, s5_lambda_re, s5_lambda_im, s5_b_re, s5_b_im, s5_c_re, s5_c_im, s5_d, s5_log_dt, s5_glu_w, s5_glu_b, diff_q_norm_g, diff_k_norm_g, diff_lambda_q1, diff_lambda_k1, diff_lambda_q2, diff_lambda_k2, diff_subln_g, w_branch_gla, w_branch_s5, w_branch_diff, w_out, norm2_g, ffn_w_up, ffn_conv_w, ffn_conv_b, ffn_w_down):
    depth, d = norm1_g.shape
    mod = _modulation(c, ada_w, ada_b)

    off_dv = _OFF_DQ + 2 * DIFF_HEADS * 2 * DIFF_DH
    w_bf = w_in.astype(BF16)
    w_all = jnp.concatenate(
        [w_bf[:, :, _OFF_GATES:], w_bf[:, :, _OFF_GLA:_OFF_GLR], w_bf[:, :, _OFF_SU:_OFF_GATES]],
        axis=2)
    w_lr = jnp.pad(w_in[:, :, _OFF_GLR:_OFF_SU], ((0, 0), (0, 0), (0, LANE - GLA_LR))).astype(BF16)
    w_qvt = jnp.swapaxes(jnp.concatenate(
        [w_in[:, :, _OFF_DQ:_OFF_DQ + _W_TILE], w_in[:, :, off_dv:_OFF_GATES]], axis=2), 1, 2).astype(BF16)
    w_a, w_b, w_c = (w.astype(BF16) for w in (w_branch_gla, w_branch_s5, w_branch_diff))
    w_o = w_out.astype(BF16)
    w_up, w_dn = ffn_w_up.astype(BF16), ffn_w_down.astype(BF16)
    glu_w = s5_glu_w.astype(BF16)
    g1, g2 = norm1_g.reshape(depth, 1, d), norm2_g.reshape(depth, 1, d)
    qk_gain = diff_q_norm_g * diff_k_norm_g
    kg = jnp.tile(qk_gain, (1, 2)).reshape(depth, 1, LANE)
    first_groups = _first_key_groups(qk_gain, x.shape[1], min(TQ, x.shape[1]), min(TK, x.shape[1]))
    sg = diff_subln_g.reshape(depth, 1, DIFF_DV)
    lam_rows = jnp.pad(
        jnp.stack([diff_lambda_q1, diff_lambda_k1, diff_lambda_q2, diff_lambda_k2], axis=1),
        ((0, 0), (0, 4), (0, LANE - DIFF_DH)))
    seq = x.shape[1]
    s5_tabs = _s5_tables(s5_lambda_re, s5_lambda_im, s5_b_re, s5_b_im, s5_c_re, s5_c_im, s5_log_dt,
                         min(TS, seq) // _S5_SEG)
    s5_d3 = s5_d.reshape(depth, 1, S5_WIDTH)
    glu_b3 = s5_glu_b.reshape(depth, 1, S5_WIDTH)
    conv_b3 = ffn_conv_b.reshape(depth, 1, -1)

    for l in range(depth):
        lambda_init = 0.8 - 0.6 * math.exp(-0.3 * l)
        p_gla = _inproj_gla(x, mod, l, g1, w_all, w_lr)
        ku, qt, vt = _inproj_qkuv(x, mod, l, g1, w_all, w_qvt, kg)
        o_a = _gla(p_gla, gla_gk_w[l], gla_gk_b[l], gla_norm_g[l])
        o_b = _s5(ku, l, s5_tabs, s5_d3, glu_w, glu_b3)
        o_c = _diff_attention(ku, qt, vt, l, first_groups, lam_rows, sg, lambda_init)
        x = _merge(x, mod, l, g1, w_all, o_a, o_b, o_c, w_a, w_b, w_c, w_o)
        x = _ffn(x, mod, l, g2, w_up, ffn_conv_w, conv_b3, w_dn)
    return x
```

```python
import functools
import math

import jax
import jax.numpy as jnp
import numpy as np
from jax import lax
from jax.experimental import pallas as pl
from jax.experimental.pallas import tpu as pltpu

F32 = jnp.float32
BF16 = jnp.bfloat16

D_MODEL = 1024
GLA_HEADS = 4
GLA_DK = 64
GLA_DV = 128
GLA_LR = 16
GLA_TAU = 16.0
GLA_CHUNK = 64
GLA_QK = GLA_HEADS * GLA_DK
GLA_V = GLA_HEADS * GLA_DV
GLA_COLS = 2 * GLA_QK + 2 * GLA_V + 128
S5_WIDTH = 512
S5_GROUP = 16
S5_GROUPS = 32
S5_STATE = 64
S5_LANES = S5_GROUPS * S5_STATE
DIFF_HEADS = 4
DIFF_DH = 64
DIFF_DV = 128
D_FF = 2816
EPS = 1e-6
LANE = 128
NEG_BIG = -1e30

TM_IN = 1024
TG = 256
TS = 512
TQ = 1024
TK = 512
TM_MERGE = 512
TM_FFN = 512
TF = 256

_OFF_GLA = 0
_OFF_GLR = 1536
_OFF_SU = 1552
_OFF_DQ = 2064
_OFF_GATES = 3600
_IN_COLS = 6672

_W_GATES = 3 * D_MODEL
_W_GLA = 2 * GLA_QK + 2 * GLA_V
_W_GLA_BLK = _W_GATES // _W_GLA
_W_TILE = 512
_W_U_BLK = (_W_GATES + _W_GLA) // _W_TILE


def _dot(a, b):
    return jnp.dot(a, b, preferred_element_type=F32)


def _dot_nt(a, b):
    return lax.dot_general(a, b, (((1,), (1,)), ((), ())), preferred_element_type=F32)


def _dot_tn(a, b):
    return lax.dot_general(a, b, (((0,), (0,)), ((), ())), preferred_element_type=F32)


def _norm_mod(x, g, sc, sh):
    ms = jnp.mean(x * x, axis=-1, keepdims=True)
    y = x * lax.rsqrt(ms + EPS) * g
    return y * (1.0 + sc) + sh


def _mod_rows(mod_ref, b, k):
    return mod_ref[0, pl.ds(b, 1), k * D_MODEL:(k + 1) * D_MODEL]


def _mod_kernel(c_ref, w_ref, b_ref, o_ref):
    c = c_ref[...]
    ca = (c * jax.nn.sigmoid(c)).astype(BF16)
    o_ref[0] = _dot(ca, w_ref[0].astype(BF16)) + b_ref[0]


def _modulation(c, ada_w, ada_b):
    depth, d, n = ada_w.shape
    bsz = c.shape[0]
    cp = jnp.zeros((8, d), F32).at[:bsz].set(c)
    tn = 1536
    return pl.pallas_call(
        _mod_kernel,
        out_shape=jax.ShapeDtypeStruct((depth, 8, n), F32),
        grid=(depth, n // tn),
        in_specs=[
            pl.BlockSpec((8, d), lambda l, j: (0, 0)),
            pl.BlockSpec((1, d, tn), lambda l, j: (l, 0, j)),
            pl.BlockSpec((1, 1, tn), lambda l, j: (l, 0, j)),
        ],
        out_specs=pl.BlockSpec((1, 8, tn), lambda l, j: (l, 0, j)),
        name="adaln_mod",
    )(cp, ada_w, ada_b.reshape(depth, 1, n))


def _seg_rmsnorm(a, g128, scale):
    lane = lax.broadcasted_iota(jnp.int32, (1, LANE), 1)
    lo = (lane < DIFF_DH).astype(F32)
    hi = 1.0 - lo
    outs = []
    for hb in range(a.shape[-1] // LANE):
        xh = a[:, hb * LANE:(hb + 1) * LANE]
        sq = xh * xh
        s_lo = jnp.sum(sq * lo, axis=-1, keepdims=True)
        s_hi = jnp.sum(sq * hi, axis=-1, keepdims=True)
        ms = (s_lo * lo + s_hi * hi) * (1.0 / DIFF_DH)
        outs.append(xh * lax.rsqrt(ms + EPS) * g128 * scale)
    return jnp.concatenate(outs, axis=-1)


def _inproj_gla_kernel(x_ref, mod_ref, g_ref, w_ref, wlr_ref, o_ref):
    b = pl.program_id(0)
    h = _norm_mod(x_ref[0], g_ref[0], _mod_rows(mod_ref, b, 1), _mod_rows(mod_ref, b, 0)).astype(BF16)
    o_ref[0, :, 0:_W_GLA] = _dot(h, w_ref[0])
    o_ref[0, :, _W_GLA:] = _dot(h, wlr_ref[0])


def _inproj_gla(x, mod, layer, g, w_all, w_lr):
    bsz, s, d = x.shape
    tm = min(TM_IN, s)
    return pl.pallas_call(
        _inproj_gla_kernel,
        out_shape=jax.ShapeDtypeStruct((bsz, s, GLA_COLS), F32),
        grid=(bsz, s // tm),
        in_specs=[
            pl.BlockSpec((1, tm, d), lambda b, i: (b, i, 0)),
            pl.BlockSpec((1, 8, mod.shape[2]), lambda b, i: (layer, 0, 0)),
            pl.BlockSpec((1, 1, d), lambda b, i: (layer, 0, 0)),
            pl.BlockSpec((1, d, _W_GLA), lambda b, i: (layer, 0, _W_GLA_BLK)),
            pl.BlockSpec((1, d, LANE), lambda b, i: (layer, 0, 0)),
        ],
        out_specs=pl.BlockSpec((1, tm, GLA_COLS), lambda b, i: (b, i, 0)),
        compiler_params=pltpu.CompilerParams(
            dimension_semantics=("arbitrary", "arbitrary"), vmem_limit_bytes=48 * 1024 * 1024),
        name="inproj_gla",
    )(x, mod, g, w_all, w_lr)


def _inproj_qkuv_kernel(x_ref, mod_ref, g_ref, wu_ref, wk_ref, wqvt_ref, kg_ref, o_ref, qt_ref, vt_ref):
    b = pl.program_id(0)
    h = _norm_mod(x_ref[0], g_ref[0], _mod_rows(mod_ref, b, 1), _mod_rows(mod_ref, b, 0)).astype(BF16)
    tn = _W_TILE
    o_ref[0, :, 0:tn] = _seg_rmsnorm(_dot(h, wk_ref[0]), kg_ref[0], 1.0).astype(o_ref.dtype)
    o_ref[0, :, tn:] = _dot(h, wu_ref[0]).astype(o_ref.dtype)
    qvt = _dot_nt(wqvt_ref[0], h)
    qscale = DIFF_DH ** -0.5 * math.log2(math.e)
    for seg in range(tn // DIFF_DH):
        rows = slice(seg * DIFF_DH, (seg + 1) * DIFF_DH)
        qs = qvt[rows]
        ms = jnp.mean(qs * qs, axis=0, keepdims=True)
        qt_ref[0, 0, rows, :] = (qs * (lax.rsqrt(ms + EPS) * qscale)).astype(qt_ref.dtype)
    vt = qvt[tn:].astype(vt_ref.dtype)
    tk = vt_ref.shape[3]
    for n in range(vt_ref.shape[1]):
        vt_ref[0, n] = vt[:, n * tk:(n + 1) * tk]


def _inproj_qkuv(x, mod, layer, g, w_all, w_qvt, k_gain):
    bsz, s, d = x.shape
    tm = min(TQ, s)
    tk = min(TK, s)
    tn = _W_TILE
    resident = lambda shape, *idx: pl.BlockSpec((1,) + shape, lambda b, i: (layer,) + idx,
                                                pipeline_mode=pl.Buffered(1))
    return pl.pallas_call(
        _inproj_qkuv_kernel,
        out_shape=(jax.ShapeDtypeStruct((bsz, s, 2 * tn), BF16),
                   jax.ShapeDtypeStruct((bsz, s // tm, tn, tm), BF16),
                   jax.ShapeDtypeStruct((bsz, s // tk, tn, tk), BF16)),
        grid=(bsz, s // tm),
        in_specs=[
            pl.BlockSpec((1, tm, d), lambda b, i: (b, i, 0)),
            pl.BlockSpec((1, 8, mod.shape[2]), lambda b, i: (layer, 0, 0)),
            resident((1, d), 0, 0),
            resident((d, tn), 0, _W_U_BLK), resident((d, tn), 0, _W_U_BLK + 2),
            resident((2 * tn, d), 0, 0), resident((1, LANE), 0, 0),
        ],
        out_specs=(
            pl.BlockSpec((1, tm, 2 * tn), lambda b, i: (b, i, 0)),
            pl.BlockSpec((1, 1, tn, tm), lambda b, i: (b, i, 0, 0)),
            pl.BlockSpec((1, tm // tk, tn, tk), lambda b, i: (b, i, 0, 0)),
        ),
        compiler_params=pltpu.CompilerParams(
            dimension_semantics=("arbitrary", "arbitrary"), vmem_limit_bytes=48 * 1024 * 1024),
        name="inproj_qkuv",
    )(x, mod, g, w_all, w_all, w_qvt, k_gain)


def _gla_kernel(p_ref, gkw_ref, gkb_ref, ng_ref, ltri_ref, lall_ref, o_ref, s_scr, *, nchunk):
    @pl.when(pl.program_id(1) == 0)
    def _():
        s_scr[...] = jnp.zeros_like(s_scr)

    blk = p_ref[0]
    q = blk[:, 0:GLA_QK]
    k = blk[:, GLA_QK:2 * GLA_QK]
    v = blk[:, 2 * GLA_QK:2 * GLA_QK + GLA_V]
    r = blk[:, 2 * GLA_QK + GLA_V:2 * GLA_QK + 2 * GLA_V]
    lr = blk[:, 2 * GLA_QK + 2 * GLA_V:]
    z = _dot(lr.astype(BF16), gkw_ref[...]) + gkb_ref[...]
    la = (jnp.minimum(z, 0.0) - jnp.log1p(jnp.exp(-jnp.abs(z)))) * (1.0 / GLA_TAU)
    la_hi = la.astype(BF16)
    la_lo = (la - la_hi.astype(F32)).astype(BF16)
    ltri = ltri_ref[...]
    lall = lall_ref[...]
    bcum = _dot(ltri, la_hi) + _dot(ltri, la_lo)
    btot = _dot(lall, la_hi) + _dot(lall, la_lo)
    qd = q * (GLA_DK ** -0.5) * jnp.exp(bcum)
    kd = (k * jnp.exp(-bcum)).astype(BF16)
    ke = (k * jnp.exp(btot - bcum)).astype(BF16)
    vb = v.astype(BF16)
    gate = r * jax.nn.sigmoid(r)

    lane_head = lax.broadcasted_iota(jnp.int32, (1, GLA_QK), 1) // GLA_DK
    rowi = lax.broadcasted_iota(jnp.int32, (GLA_QK, GLA_CHUNK), 0) % GLA_CHUNK
    colj = lax.broadcasted_iota(jnp.int32, (GLA_QK, GLA_CHUNK), 1)
    tril = rowi >= colj
    ones_cv = jnp.ones((GLA_CHUNK, GLA_DV), BF16)
    ng = ng_ref[...]

    chunks = [slice(c * GLA_CHUNK, (c + 1) * GLA_CHUNK) for c in range(nchunk)]

    states = []
    state = s_scr[...]
    for rows in chunks:
        states.append(state.astype(BF16))
        kv = _dot_tn(ke[rows], vb[rows])
        kvd = jnp.concatenate(
            [kv[h * GLA_DK:(h + 1) * GLA_DK, h * GLA_DV:(h + 1) * GLA_DV] for h in range(GLA_HEADS)],
            axis=0)
        ldec = _dot_tn(la_hi[rows], ones_cv) + _dot_tn(la_lo[rows], ones_cv)
        state = jnp.exp(ldec) * state + kvd
    s_scr[...] = state

    qms = [jnp.concatenate(
        [jnp.where(lane_head == h, qd[rows], 0.0).astype(BF16) for h in range(GLA_HEADS)], axis=0)
        for rows in chunks]
    atts = [_dot_nt(qm, kd[rows]) for qm, rows in zip(qms, chunks)]
    inters = [_dot(qm, st) for qm, st in zip(qms, states)]
    atts = [jnp.where(tril, a, 0.0).astype(BF16) for a in atts]
    heads = [slice(h * GLA_CHUNK, (h + 1) * GLA_CHUNK) for h in range(GLA_HEADS)]
    outs = [[_dot(att[hr], vb[rows][:, h * GLA_DV:(h + 1) * GLA_DV]) + inter[hr]
             for h, hr in enumerate(heads)]
            for att, inter, rows in zip(atts, inters, chunks)]
    for o_heads, rows in zip(outs, chunks):
        normed = [o_h * lax.rsqrt(jnp.mean(o_h * o_h, axis=-1, keepdims=True) + EPS) * ng for o_h in o_heads]
        o = jnp.concatenate(normed, axis=1) * gate[rows]
        o_ref[0, rows, :] = o.astype(o_ref.dtype)


def _gla(p, gk_w, gk_b, norm_g):
    bsz, s, _ = p.shape
    tg = min(TG, s)
    nchunk = tg // GLA_CHUNK
    gkw = jnp.zeros((LANE, GLA_QK), F32).at[:GLA_LR].set(gk_w).astype(BF16)
    ri = jnp.arange(tg)[:, None]
    ci = jnp.arange(tg)[None, :]
    same = (ri // GLA_CHUNK) == (ci // GLA_CHUNK)
    ltri = (same & (ri >= ci)).astype(BF16)
    lall = same.astype(BF16)
    const = lambda shape: pl.BlockSpec(shape, lambda b, t: (0,) * len(shape))
    return pl.pallas_call(
        functools.partial(_gla_kernel, nchunk=nchunk),
        out_shape=jax.ShapeDtypeStruct((bsz, s, GLA_V), BF16),
        grid=(bsz, s // tg),
        in_specs=[
            pl.BlockSpec((1, tg, GLA_COLS), lambda b, t: (b, t, 0)),
            const((LANE, GLA_QK)), const((1, GLA_QK)), const((1, GLA_DV)),
            const((tg, tg)), const((tg, tg)),
        ],
        out_specs=pl.BlockSpec((1, tg, GLA_V), lambda b, t: (b, t, 0)),
        scratch_shapes=[pltpu.VMEM((GLA_QK, GLA_DV), F32)],
        compiler_params=pltpu.CompilerParams(dimension_semantics=("arbitrary", "arbitrary")),
        name="gla",
    )(p, gkw, gk_b.reshape(1, GLA_QK), norm_g.reshape(1, GLA_DV), ltri, lall)


_S5_COLS = S5_LANES // LANE
_S5_CG = 8
_S5_SEG = 8


def _s5_tables(lam_re, lam_im, b_re, b_im, c_re, c_im, log_dt, sl):
    nl = lam_re.shape[0]
    dt = jnp.exp(log_dt.astype(F32))[..., None]
    ar = (lam_re.astype(F32) * dt).reshape(nl, 1, S5_LANES)
    ai = (lam_im.astype(F32) * dt).reshape(nl, 1, S5_LANES)

    def powers(ks):
        kk = jnp.asarray(ks, F32)[None, :, None]
        mag = jnp.exp(kk * ar)
        return mag * jnp.cos(kk * ai), mag * jnp.sin(kk * ai)

    lr1, li1 = powers([1.0])
    lbr, lbi = lr1.reshape(lam_re.shape), li1.reshape(lam_re.shape)
    den = lam_re * lam_re + lam_im * lam_im
    nr = ((lbr - 1.0) * lam_re + lbi * lam_im) / den
    ni = (lbi * lam_re - (lbr - 1.0) * lam_im) / den
    bbr = nr[..., None] * b_re - ni[..., None] * b_im
    bbi = nr[..., None] * b_im + ni[..., None] * b_re
    eye4 = jnp.eye(4, dtype=F32)
    band = eye4[np.arange(8) % 4]

    def b_tiles(bb):
        bb4 = bb.reshape(nl, 8, 4, S5_STATE, S5_GROUP)
        small = jnp.einsum('lnkph,kj->lnkhjp', bb4, eye4).reshape(nl, 8, 64, 256)
        return jnp.einsum('lnrc,nq->lnqrc', small, band).reshape(nl, 8, 256, 256)

    def c_tiles(cc):
        cc4 = cc.reshape(nl, 8, 4, S5_GROUP, S5_STATE)
        small = jnp.einsum('lnkhp,kj->lnkpjh', cc4, eye4).reshape(nl, 8, 256, 64)
        return jnp.einsum('lnrc,nq->lnrqc', small, band).reshape(nl, 8, 256, 256)

    bt = jnp.concatenate([b_tiles(bbr), b_tiles(bbi)], axis=1).astype(BF16)
    ct = jnp.concatenate([c_tiles(c_re), c_tiles(-c_im)], axis=1).astype(BF16)
    lam8 = jnp.concatenate([jnp.broadcast_to(lr1, (nl, _S5_SEG, S5_LANES)),
                            jnp.broadcast_to(li1, (nl, _S5_SEG, S5_LANES))], axis=2)
    dsteps = np.array([1, 2, 4])
    keep = jnp.asarray(np.arange(_S5_SEG)[None, :] >= dsteps[:, None], F32)
    pr3, pi3 = powers(dsteps * float(sl))
    logstep = jnp.stack([keep[None, :, :, None] * pr3[:, :, None, :],
                         keep[None, :, :, None] * pi3[:, :, None, :]], axis=2)
    logstep = logstep.transpose(0, 3, 1, 2, 4).reshape(nl, _S5_SEG, 6 * S5_LANES)
    qr, qi = powers(np.arange(1, _S5_SEG + 1) * float(sl))
    seg_tab = jnp.concatenate([logstep, qr, qi], axis=2)
    pwr, pwi = powers(np.arange(1, sl + 1))
    pw = jnp.concatenate([pwr, pwi], axis=2)
    return bt, ct, lam8, seg_tab, pw


def _s5_kernel(u_ref, bt_ref, ct_ref, lam_ref, seg_ref, pw_ref, d_ref, gw_ref, gb_ref, o_ref,
               bu_scr, c_scr, x_scr, car_scr, *, ts):
    sl = ts // _S5_SEG
    pitch = sl + 8

    @pl.when(pl.program_id(1) == 0)
    def _():
        car_scr[...] = jnp.zeros_like(car_scr)

    u = u_ref[0]
    for t in range(16):
        half = (t % 8) // 4
        res = _dot(u[:, half * 256:(half + 1) * 256], bt_ref[0, t])
        for s in range(_S5_SEG):
            for w in range(2):
                bu_scr[2 * t + w, s * pitch:s * pitch + sl, :] = res[s * sl:(s + 1) * sl, w * LANE:(w + 1) * LANE]

    def lanes(ref, idx, j):
        return ref[0, :, idx * S5_LANES + j * LANE: idx * S5_LANES + (j + 1) * LANE]

    for j0 in range(0, _S5_COLS, _S5_CG):
        cols = list(range(j0, j0 + _S5_CG))

        def body(t, carry, cols=cols):
            new = [None] * (2 * len(cols))
            for n, j in enumerate(cols):
                xr, xi = carry[n], carry[len(cols) + n]
                lr_, li_ = lanes(lam_ref, 0, j), lanes(lam_ref, 1, j)
                vr = bu_scr[j, pl.ds(t, _S5_SEG, stride=pitch), :]
                vi = bu_scr[_S5_COLS + j, pl.ds(t, _S5_SEG, stride=pitch), :]
                nr = lr_ * xr - li_ * xi + vr
                ni = lr_ * xi + li_ * xr + vi
                bu_scr[j, pl.ds(t, _S5_SEG, stride=pitch), :] = nr
                bu_scr[_S5_COLS + j, pl.ds(t, _S5_SEG, stride=pitch), :] = ni
                new[n], new[len(cols) + n] = nr, ni
            return tuple(new)

        zero = jnp.zeros((_S5_SEG, LANE), F32)
        fin = lax.fori_loop(0, sl, body, (zero,) * (2 * len(cols)), unroll=4)

        row = lax.broadcasted_iota(jnp.int32, (_S5_SEG, LANE), 0)
        for n, j in enumerate(cols):
            gr, gi = fin[n], fin[len(cols) + n]
            for si, dstep in enumerate((1, 2, 4)):
                tr, ti = lanes(seg_ref, 2 * si, j), lanes(seg_ref, 2 * si + 1, j)
                sr = pltpu.roll(gr, dstep, 0)
                sm = pltpu.roll(gi, dstep, 0)
                gr, gi = gr + tr * sr - ti * sm, gi + tr * sm + ti * sr
            cr = car_scr[:, j * LANE:(j + 1) * LANE]
            ci = car_scr[:, S5_LANES + j * LANE:S5_LANES + (j + 1) * LANE]
            qr, qi = lanes(seg_ref, 6, j), lanes(seg_ref, 7, j)
            gr, gi = gr + qr * cr - qi * ci, gi + qr * ci + qi * cr
            c_scr[j] = jnp.where(row == 0, cr, pltpu.roll(gr, 1, 0))
            c_scr[_S5_COLS + j] = jnp.where(row == 0, ci, pltpu.roll(gi, 1, 0))
            car_scr[:, j * LANE:(j + 1) * LANE] = jnp.broadcast_to(gr[_S5_SEG - 1:, :], (_S5_SEG, LANE))
            car_scr[:, S5_LANES + j * LANE:S5_LANES + (j + 1) * LANE] = jnp.broadcast_to(
                gi[_S5_SEG - 1:, :], (_S5_SEG, LANE))

    for j in range(_S5_COLS):
        pr_ = pw_ref[0, :, j * LANE:(j + 1) * LANE]
        pi_ = pw_ref[0, :, S5_LANES + j * LANE:S5_LANES + (j + 1) * LANE]
        for s in range(_S5_SEG):
            cr = c_scr[j, s:s + 1, :]
            ci = c_scr[_S5_COLS + j, s:s + 1, :]
            xr = bu_scr[j, s * pitch:s * pitch + sl, :] + pr_ * cr - pi_ * ci
            xi = bu_scr[_S5_COLS + j, s * pitch:s * pitch + sl, :] + pr_ * ci + pi_ * cr
            x_scr[s * sl:(s + 1) * sl, j * LANE:(j + 1) * LANE] = xr.astype(BF16)
            x_scr[s * sl:(s + 1) * sl, S5_LANES + j * LANE:S5_LANES + (j + 1) * LANE] = xi.astype(BF16)

    ys = []
    for half in range(2):
        acc = None
        for t in [half * 4 + i for i in range(4)] + [8 + half * 4 + i for i in range(4)]:
            part = _dot(x_scr[:, t * 256:(t + 1) * 256], ct_ref[0, t])
            acc = part if acc is None else acc + part
        ys.append(acc)
    y = jnp.concatenate(ys, axis=1) + d_ref[0] * u.astype(F32)
    g = 0.5 * y * (1.0 + jnp.tanh(math.sqrt(2.0 / math.pi) * (y + 0.044715 * (y * y * y))))
    zz = _dot(g.astype(BF16), gw_ref[0]) + gb_ref[0]
    o_ref[0] = (g * jax.nn.sigmoid(zz)).astype(o_ref.dtype)


def _s5(qku, layer, tables, d_skip, glu_w, glu_b):
    bsz, s, _ = qku.shape
    ts = min(TS, s)
    sl = ts // _S5_SEG
    bt, ct, lam8, seg_tab, pw = tables
    const = lambda shape: pl.BlockSpec((1,) + shape, lambda b, t: (layer,) + (0,) * len(shape))
    return pl.pallas_call(
        functools.partial(_s5_kernel, ts=ts),
        out_shape=jax.ShapeDtypeStruct((bsz, s, S5_WIDTH), BF16),
        grid=(bsz, s // ts),
        in_specs=[
            pl.BlockSpec((1, ts, S5_WIDTH), lambda b, t: (b, t, 1)),
            const((16, 256, 256)), const((16, 256, 256)),
            const((_S5_SEG, 2 * S5_LANES)), const((_S5_SEG, 8 * S5_LANES)), const((sl, 2 * S5_LANES)),
            const((1, S5_WIDTH)), const((S5_WIDTH, S5_WIDTH)), const((1, S5_WIDTH)),
        ],
        out_specs=pl.BlockSpec((1, ts, S5_WIDTH), lambda b, t: (b, t, 0)),
        scratch_shapes=[
            pltpu.VMEM((2 * _S5_COLS, _S5_SEG * (sl + 8), LANE), F32),
            pltpu.VMEM((2 * _S5_COLS, _S5_SEG, LANE), F32),
            pltpu.VMEM((ts, 2 * S5_LANES), BF16),
            pltpu.VMEM((_S5_SEG, 2 * S5_LANES), F32)],
        compiler_params=pltpu.CompilerParams(
            dimension_semantics=("arbitrary", "arbitrary"), vmem_limit_bytes=48 * 1024 * 1024),
        name="s5",
    )(qku, bt, ct, lam8, seg_tab, pw, d_skip, glu_w, glu_b)


_ATT_STRIP = 256
_ATT_AHEAD = 3
_ATT_ONES = 16


def _alibi_tables(tq, tk):
    parts, rem = [], math.log2(math.e)
    for _ in range(3):
        p = float(np.asarray(rem, dtype=BF16))
        parts.append(p)
        rem -= p
    qa = ((np.arange(tq) // 64) * 64).astype(np.float32)
    qb = (np.arange(tq) % 64).astype(np.float32)
    ka = ((np.arange(tk) // 64) * 64).astype(np.float32)
    kb = (np.arange(tk) % 64).astype(np.float32)
    qx = np.zeros((DIFF_HEADS, tq, LANE), np.float32)
    kx = np.zeros((DIFF_HEADS, tk, LANE), np.float32)
    cval = np.zeros((DIFF_HEADS, 8, LANE), np.float32)
    for h in range(DIFF_HEADS):
        slope = 2.0 ** (-8.0 * (h + 1) / DIFF_HEADS)
        for n, p in enumerate(parts):
            qx[h, :, n] = qx[h, :, 3 + n] = slope * p
            kx[h, :, n] = ka
            kx[h, :, 3 + n] = kb
            qx[h, :, 6 + n] = -qa
            qx[h, :, 9 + n] = -qb
            kx[h, :, 6 + n] = kx[h, :, 9 + n] = slope * p
        cval[h] = slope * math.log2(math.e)
    qxt = np.ascontiguousarray(qx.transpose(0, 2, 1))
    return jnp.asarray(qxt, BF16), jnp.asarray(kx, BF16), jnp.asarray(cval, F32)


def _attn_kernel(lo_ref, q_ref, k_ref, vt_ref, qx_ref, kx_ref, c_ref, lam_ref, sg_ref, o_ref,
                 qt_scr, m_scr, acc_scr, *, tq, tk, lambda_init, layer):
    qi = pl.program_id(2)
    lo = lo_ref[layer, pl.program_id(1), qi]
    c11 = c_ref[0, 0:1, 0:1]
    qt = q_ref[0, 0]
    row = lax.broadcasted_iota(jnp.int32, (LANE, 1), 0)
    zero = jnp.zeros_like(qt)
    for half in range(2):
        qt_scr[half, 0:LANE, :] = jnp.where((row >= DIFF_DH) == (half == 1), qt, zero)
        qt_scr[half, LANE:, :] = qx_ref[0]
    kx = kx_ref[0]
    nstrip = tq // _ATT_STRIP
    kpq = tq // tk
    ones_rows = jnp.ones((_ATT_ONES, tk), BF16)

    m_scr[...] = jnp.full_like(m_scr, NEG_BIG)
    acc_scr[...] = jnp.zeros_like(acc_scr)

    def blocks(entries):
        loaded = []
        for kj, diag in entries:
            start = pl.multiple_of(kj * tk, tk)
            ke = jnp.concatenate([k_ref[0, pl.ds(start, tk), :], kx], axis=1)
            vt = jnp.concatenate([vt_ref[0, kj], ones_rows], axis=0)
            shift = c11 * (kj * tk - qi * tq).astype(F32)
            loaded.append((ke, vt, shift))
        units = []
        for e, (kj, diag) in enumerate(entries):
            for half in range(2):
                for st in range(nstrip):
                    nk = tk if diag is None else min(tk, (st + 1) * _ATT_STRIP - diag * tk)
                    if nk > 0:
                        units.append((e, half, st, nk))

        def scores(unit):
            e, half, st, nk = unit
            return _dot(loaded[e][0][:nk], qt_scr[half, :, st * _ATT_STRIP:(st + 1) * _ATT_STRIP])

        pending = [scores(u) for u in units[:_ATT_AHEAD]]
        for n, (e, half, st, nk) in enumerate(units):
            s_t = pending.pop(0)
            if n + _ATT_AHEAD < len(units):
                pending.append(scores(units[n + _ATT_AHEAD]))
            _, vt, shift = loaded[e]
            diag = entries[e][1]
            cols = slice(st * _ATT_STRIP, (st + 1) * _ATT_STRIP)
            if diag is not None and diag * tk + nk - 1 > st * _ATT_STRIP:
                jrow = lax.broadcasted_iota(jnp.int32, (nk, _ATT_STRIP), 0)
                icol = lax.broadcasted_iota(jnp.int32, (nk, _ATT_STRIP), 1)
                s_t = jnp.where(jrow + diag * tk <= icol + st * _ATT_STRIP, s_t, NEG_BIG)
            m_old = m_scr[half, :, cols] - shift
            m_new = jnp.maximum(m_old, jnp.max(s_t, axis=0, keepdims=True))
            alpha = jnp.exp2(m_old - m_new)
            p = jnp.exp2(s_t - m_new)
            acc_scr[half, :, cols] = alpha * acc_scr[half, :, cols] + _dot(vt[:, :nk], p.astype(BF16))
            m_scr[half, :, cols] = m_new + shift

    def body(i, carry):
        blocks([(i * kpq + d, None) for d in range(kpq)])
        return carry

    if kpq == 2:
        @pl.when(lo % 2 == 1)
        def _():
            blocks([(lo, None)])

        first_group = (lo + 1) // 2
    else:
        first_group = lo // kpq
    lax.fori_loop(first_group, qi, body, 0)
    blocks([(qi * kpq + d, d) for d in range(kpq)])

    lam = (jnp.exp(jnp.sum(lam_ref[0, 0:1, :] * lam_ref[0, 1:2, :], axis=-1, keepdims=True))
           - jnp.exp(jnp.sum(lam_ref[0, 2:3, :] * lam_ref[0, 3:4, :], axis=-1, keepdims=True)) + lambda_init)
    norm = [acc_scr[half, 0:DIFF_DV, :] / acc_scr[half, DIFF_DV:DIFF_DV + 1, :] for half in range(2)]
    o_t = norm[0] - lam * norm[1]
    o = o_t.T
    ms = jnp.mean(o * o, axis=-1, keepdims=True)
    o = o * lax.rsqrt(ms + EPS) * sg_ref[0] * (1.0 - lambda_init)
    o_ref[0] = o.astype(o_ref.dtype)


_F32_EXP2_ZERO = 152.0


def _first_key_groups(qk_gain, seq, tq, tk):
    gmax = jnp.max(jnp.abs(qk_gain), axis=1)
    qnorm = 8.0 * DIFF_DH ** -0.5 * math.log2(math.e)
    bound = 1.02 * qnorm * 8.0 * gmax
    thresh = (2.0 * bound + _F32_EXP2_ZERO)[:, None, None]
    slopes = 2.0 ** (-8.0 * np.arange(1, DIFF_HEADS + 1) / DIFF_HEADS)
    cvals = jnp.asarray(slopes * math.log2(math.e), F32)[None, :, None]
    q0 = jnp.asarray(np.arange(seq // tq) * tq, F32)[None, None, :]
    kj_min = jnp.ceil((q0 + 1.0 - thresh / cvals) / tk - 1.0)
    kj_min = jnp.clip(kj_min, 0.0, float(seq // tk)).astype(jnp.int32)
    return jnp.minimum(kj_min, (q0 / tk).astype(jnp.int32))


def _diff_attention(ku, qt, vt, layer, first_groups, lam_rows, subln_g, lambda_init):
    bsz, s, _ = ku.shape
    tk = vt.shape[3]
    tq = qt.shape[3]
    qx, kx, cval = _alibi_tables(tq, tk)
    return pl.pallas_call(
        functools.partial(_attn_kernel, tq=tq, tk=tk, lambda_init=lambda_init, layer=layer),
        out_shape=jax.ShapeDtypeStruct((bsz, s, DIFF_HEADS * DIFF_DV), BF16),
        grid=(bsz, DIFF_HEADS, s // tq),
        in_specs=[
            pl.BlockSpec(memory_space=pltpu.SMEM),
            pl.BlockSpec((1, 1, LANE, tq), lambda b, h, i: (b, i, h, 0)),
            pl.BlockSpec((1, s, LANE), lambda b, h, i: (b, 0, h)),
            pl.BlockSpec((1, s // tk, DIFF_DV, tk), lambda b, h, i: (b, 0, h, 0)),
            pl.BlockSpec((1, LANE, tq), lambda b, h, i: (h, 0, 0)),
            pl.BlockSpec((1, tk, LANE), lambda b, h, i: (h, 0, 0)),
            pl.BlockSpec((1, 8, LANE), lambda b, h, i: (h, 0, 0)),
            pl.BlockSpec((1, 8, LANE), lambda b, h, i: (layer, 0, 0)),
            pl.BlockSpec((1, 1, LANE), lambda b, h, i: (layer, 0, 0)),
        ],
        out_specs=pl.BlockSpec((1, tq, LANE), lambda b, h, i: (b, i, h)),
        scratch_shapes=[
            pltpu.VMEM((2, 2 * LANE, tq), BF16), pltpu.VMEM((2, 1, tq), F32),
            pltpu.VMEM((2, DIFF_DV + _ATT_ONES, tq), F32)],
        compiler_params=pltpu.CompilerParams(
            dimension_semantics=("arbitrary", "arbitrary", "arbitrary"),
            vmem_limit_bytes=48 * 1024 * 1024),
        name="diff_attn",
    )(first_groups, qt, ku, vt, qx, kx, cval, lam_rows, subln_g)


def _merge_kernel(x_ref, mod_ref, g_ref, wg_ref, oa_ref, ob_ref, oc_ref, wa_ref, wb_ref, wc_ref,
                  wo_ref, o_ref):
    b = pl.program_id(0)
    x = x_ref[0]
    h = _norm_mod(x, g_ref[0], _mod_rows(mod_ref, b, 1), _mod_rows(mod_ref, b, 0)).astype(BF16)
    merged = None
    for n, (ob, wb) in enumerate(((oa_ref, wa_ref), (ob_ref, wb_ref), (oc_ref, wc_ref))):
        gate = jax.nn.sigmoid(_dot(h, wg_ref[0, :, n * D_MODEL:(n + 1) * D_MODEL]))
        term = gate * _dot(ob[0], wb[0])
        merged = term if merged is None else merged + term
    y = _dot(merged.astype(BF16), wo_ref[0])
    o_ref[0] = x + _mod_rows(mod_ref, b, 2) * y


def _merge(x, mod, layer, g, w_all, o_a, o_b, o_c, w_a, w_b, w_c, w_out):
    bsz, s, d = x.shape
    tm = min(TM_MERGE, s)
    lay = lambda shape: pl.BlockSpec((1,) + shape, lambda b, i: (layer,) + (0,) * len(shape))
    tok = lambda n: pl.BlockSpec((1, tm, n), lambda b, i: (b, i, 0))
    return pl.pallas_call(
        _merge_kernel,
        out_shape=jax.ShapeDtypeStruct((bsz, s, d), F32),
        grid=(bsz, s // tm),
        in_specs=[
            tok(d),
            lay((8, mod.shape[2])),
            lay((1, d)), lay((d, _W_GATES)),
            tok(GLA_V), tok(S5_WIDTH), tok(DIFF_HEADS * DIFF_DV),
            lay((GLA_V, d)), lay((S5_WIDTH, d)), lay((DIFF_HEADS * DIFF_DV, d)), lay((d, d)),
        ],
        out_specs=tok(d),
        compiler_params=pltpu.CompilerParams(
            dimension_semantics=("arbitrary", "arbitrary"), vmem_limit_bytes=56 * 1024 * 1024),
        name="merge",
    )(x, mod, g, w_all, o_a, o_b, o_c, w_a, w_b, w_c, w_out)


_HALO = 16


def _ffn_kernel(x_ref, mod_ref, g_ref, wup_ref, cw_ref, cb_ref, wd_ref,
                o_ref, h_scr, halo_scr, up_scr, act_scr, *, tm, tf, f):
    b = pl.program_id(0)

    @pl.when(pl.program_id(1) == 0)
    def _():
        halo_scr[...] = jnp.zeros_like(halo_scr)

    x = x_ref[0]
    h = _norm_mod(x, g_ref[0], _mod_rows(mod_ref, b, 4), _mod_rows(mod_ref, b, 3)).astype(BF16)
    h_scr[0:_HALO, :] = halo_scr[...]
    h_scr[_HALO:, :] = h
    halo_scr[...] = h[tm - _HALO:, :]
    hh = h_scr[...]

    def conv(part, col):
        cw = cw_ref[0, :, col:col + tf]
        taps = [cw[n:n + 1, :] * up_scr[part, pl.ds(_HALO - 2 + n, tm), :] for n in range(3)]
        return taps[0] + taps[1] + taps[2] + cb_ref[0, :, col:col + tf]

    for c in range(f // tf):
        for part in range(2):
            col = part * f + c * tf
            up_scr[part] = _dot(hh, wup_ref[0, :, col:col + tf])
        a = conv(0, c * tf)
        gg = conv(1, f + c * tf)
        act_scr[:, c * tf:(c + 1) * tf] = (a * jax.nn.sigmoid(a) * gg).astype(BF16)

    o_ref[0] = x + _mod_rows(mod_ref, b, 5) * _dot(act_scr[...], wd_ref[0])


def _ffn(x, mod, layer, g, w_up, conv_w, conv_b, w_down):
    bsz, s, d = x.shape
    f = w_down.shape[1]
    tm = min(TM_FFN, s)
    tf = TF
    resident = lambda shape: pl.BlockSpec((1,) + shape, lambda b, i: (layer,) + (0,) * len(shape),
                                          pipeline_mode=pl.Buffered(1))
    return pl.pallas_call(
        functools.partial(_ffn_kernel, tm=tm, tf=tf, f=f),
        out_shape=jax.ShapeDtypeStruct((bsz, s, d), F32),
        grid=(bsz, s // tm),
        in_specs=[
            pl.BlockSpec((1, tm, d), lambda b, i: (b, i, 0)),
            pl.BlockSpec((1, 8, mod.shape[2]), lambda b, i: (layer, 0, 0)),
            resident((1, d)), resident((d, 2 * f)), resident((3, 2 * f)), resident((1, 2 * f)),
            resident((f, d)),
        ],
        out_specs=pl.BlockSpec((1, tm, d), lambda b, i: (b, i, 0)),
        scratch_shapes=[
            pltpu.VMEM((tm + _HALO, d), BF16), pltpu.VMEM((_HALO, d), BF16),
            pltpu.VMEM((2, tm + _HALO, tf), F32), pltpu.VMEM((tm, f), BF16)],
        compiler_params=pltpu.CompilerParams(
            dimension_semantics=("arbitrary", "arbitrary"), vmem_limit_bytes=56 * 1024 * 1024),
        name="ffn",
    )(x, mod, g, w_up, conv_w, conv_b, w_down)


def kernel(x, c, ada_w, ada_b, norm1_g, w_in, gla_gk_w, gla_gk_b, gla_norm_g, s5_lambda_re, s5_lambda_im, s5_b_re, s5_b_im, s5_c_re, s5_c_im, s5_d, s5_log_dt, s5_glu_w, s5_glu_b, diff_q_norm_g, diff_k_norm_g, diff_lambda_q1, diff_lambda_k1, diff_lambda_q2, diff_lambda_k2, diff_subln_g, w_branch_gla, w_branch_s5, w_branch_diff, w_out, norm2_g, ffn_w_up, ffn_conv_w, ffn_conv_b, ffn_w_down):
    depth, d = norm1_g.shape
    mod = _modulation(c, ada_w, ada_b)

    off_dv = _OFF_DQ + 2 * DIFF_HEADS * 2 * DIFF_DH
    w_bf = w_in.astype(BF16)
    w_all = jnp.concatenate(
        [w_bf[:, :, _OFF_GATES:], w_bf[:, :, _OFF_GLA:_OFF_GLR], w_bf[:, :, _OFF_SU:_OFF_GATES]],
        axis=2)
    w_lr = jnp.pad(w_in[:, :, _OFF_GLR:_OFF_SU], ((0, 0), (0, 0), (0, LANE - GLA_LR))).astype(BF16)
    w_qvt = jnp.swapaxes(jnp.concatenate(
        [w_in[:, :, _OFF_DQ:_OFF_DQ + _W_TILE], w_in[:, :, off_dv:_OFF_GATES]], axis=2), 1, 2).astype(BF16)
    w_a, w_b, w_c = (w.astype(BF16) for w in (w_branch_gla, w_branch_s5, w_branch_diff))
    w_o = w_out.astype(BF16)
    w_up, w_dn = ffn_w_up.astype(BF16), ffn_w_down.astype(BF16)
    glu_w = s5_glu_w.astype(BF16)
    g1, g2 = norm1_g.reshape(depth, 1, d), norm2_g.reshape(depth, 1, d)
    qk_gain = diff_q_norm_g * diff_k_norm_g
    kg = jnp.tile(qk_gain, (1, 2)).reshape(depth, 1, LANE)
    first_groups = _first_key_groups(qk_gain, x.shape[1], min(TQ, x.shape[1]), min(TK, x.shape[1]))
    sg = diff_subln_g.reshape(depth, 1, DIFF_DV)
    lam_rows = jnp.pad(
        jnp.stack([diff_lambda_q1, diff_lambda_k1, diff_lambda_q2, diff_lambda_k2], axis=1),
        ((0, 0), (0, 4), (0, LANE - DIFF_DH)))
    seq = x.shape[1]
    s5_tabs = _s5_tables(s5_lambda_re, s5_lambda_im, s5_b_re, s5_b_im, s5_c_re, s5_c_im, s5_log_dt,
                         min(TS, seq) // _S5_SEG)
    s5_d3 = s5_d.reshape(depth, 1, S5_WIDTH)
    glu_b3 = s5_glu_b.reshape(depth, 1, S5_WIDTH)
    conv_b3 = ffn_conv_b.reshape(depth, 1, -1)

    for l in range(depth):
        lambda_init = 0.8 - 0.6 * math.exp(-0.3 * l)
        p_gla = _inproj_gla(x, mod, l, g1, w_all, w_lr)
        ku, qt, vt = _inproj_qkuv(x, mod, l, g1, w_all, w_qvt, kg)
        o_a = _gla(p_gla, gla_gk_w[l], gla_gk_b[l], gla_norm_g[l])
        o_b = _s5(ku, l, s5_tabs, s5_d3, glu_w, glu_b3)
        o_c = _diff_attention(ku, qt, vt, l, first_groups, lam_rows, sg, lambda_init)
        x = _merge(x, mod, l, g1, w_all, o_a, o_b, o_c, w_a, w_b, w_c, w_o)
        x = _ffn(x, mod, l, g2, w_up, ffn_conv_w, conv_b3, w_dn)
    return x
```

```python
import functools
import math

import jax
import jax.numpy as jnp
import numpy as np
from jax import lax
from jax.experimental import pallas as pl
from jax.experimental.pallas import tpu as pltpu

F32 = jnp.float32
BF16 = jnp.bfloat16

D_MODEL = 1024
GLA_HEADS = 4
GLA_DK = 64
GLA_DV = 128
GLA_LR = 16
GLA_TAU = 16.0
GLA_CHUNK = 64
GLA_QK = GLA_HEADS * GLA_DK
GLA_V = GLA_HEADS * GLA_DV
GLA_COLS = 2 * GLA_QK + 2 * GLA_V + 128
S5_WIDTH = 512
S5_GROUP = 16
S5_GROUPS = 32
S5_STATE = 64
S5_LANES = S5_GROUPS * S5_STATE
DIFF_HEADS = 4
DIFF_DH = 64
DIFF_DV = 128
D_FF = 2816
EPS = 1e-6
LANE = 128
NEG_BIG = -1e30

TM_IN = 1024
TG = 256
TS = 512
TQ = 1024
TK = 512
TM_MERGE = 512
TM_FFN = 512
TF = 256

_OFF_GLA = 0
_OFF_GLR = 1536
_OFF_SU = 1552
_OFF_DQ = 2064
_OFF_GATES = 3600
_IN_COLS = 6672

_W_GATES = 3 * D_MODEL
_W_GLA = 2 * GLA_QK + 2 * GLA_V
_W_GLA_BLK = _W_GATES // _W_GLA
_W_TILE = 512
_W_U_BLK = (_W_GATES + _W_GLA) // _W_TILE


def _dot(a, b):
    return jnp.dot(a, b, preferred_element_type=F32)


def _dot_nt(a, b):
    return lax.dot_general(a, b, (((1,), (1,)), ((), ())), preferred_element_type=F32)


def _dot_tn(a, b):
    return lax.dot_general(a, b, (((0,), (0,)), ((), ())), preferred_element_type=F32)


def _norm_mod(x, g, sc, sh):
    ms = jnp.mean(x * x, axis=-1, keepdims=True)
    y = x * lax.rsqrt(ms + EPS) * g
    return y * (1.0 + sc) + sh


def _mod_rows(mod_ref, b, k):
    return mod_ref[0, pl.ds(b, 1), k * D_MODEL:(k + 1) * D_MODEL]


def _mod_kernel(c_ref, w_ref, b_ref, o_ref):
    c = c_ref[...]
    ca = (c * jax.nn.sigmoid(c)).astype(BF16)
    o_ref[0] = _dot(ca, w_ref[0].astype(BF16)) + b_ref[0]


def _modulation(c, ada_w, ada_b):
    depth, d, n = ada_w.shape
    bsz = c.shape[0]
    cp = jnp.zeros((8, d), F32).at[:bsz].set(c)
    tn = 1536
    return pl.pallas_call(
        _mod_kernel,
        out_shape=jax.ShapeDtypeStruct((depth, 8, n), F32),
        grid=(depth, n // tn),
        in_specs=[
            pl.BlockSpec((8, d), lambda l, j: (0, 0)),
            pl.BlockSpec((1, d, tn), lambda l, j: (l, 0, j)),
            pl.BlockSpec((1, 1, tn), lambda l, j: (l, 0, j)),
        ],
        out_specs=pl.BlockSpec((1, 8, tn), lambda l, j: (l, 0, j)),
        name="adaln_mod",
    )(cp, ada_w, ada_b.reshape(depth, 1, n))


def _seg_rmsnorm(a, g128, scale):
    lane = lax.broadcasted_iota(jnp.int32, (1, LANE), 1)
    lo = (lane < DIFF_DH).astype(F32)
    hi = 1.0 - lo
    outs = []
    for hb in range(a.shape[-1] // LANE):
        xh = a[:, hb * LANE:(hb + 1) * LANE]
        sq = xh * xh
        s_lo = jnp.sum(sq * lo, axis=-1, keepdims=True)
        s_hi = jnp.sum(sq * hi, axis=-1, keepdims=True)
        ms = (s_lo * lo + s_hi * hi) * (1.0 / DIFF_DH)
        outs.append(xh * lax.rsqrt(ms + EPS) * g128 * scale)
    return jnp.concatenate(outs, axis=-1)


def _inproj_gla_kernel(x_ref, mod_ref, g_ref, w_ref, wlr_ref, o_ref):
    b = pl.program_id(0)
    h = _norm_mod(x_ref[0], g_ref[0], _mod_rows(mod_ref, b, 1), _mod_rows(mod_ref, b, 0)).astype(BF16)
    o_ref[0, :, 0:_W_GLA] = _dot(h, w_ref[0])
    o_ref[0, :, _W_GLA:] = _dot(h, wlr_ref[0])


def _inproj_gla(x, mod, layer, g, w_all, w_lr):
    bsz, s, d = x.shape
    tm = min(TM_IN, s)
    return pl.pallas_call(
        _inproj_gla_kernel,
        out_shape=jax.ShapeDtypeStruct((bsz, s, GLA_COLS), F32),
        grid=(bsz, s // tm),
        in_specs=[
            pl.BlockSpec((1, tm, d), lambda b, i: (b, i, 0)),
            pl.BlockSpec((1, 8, mod.shape[2]), lambda b, i: (layer, 0, 0)),
            pl.BlockSpec((1, 1, d), lambda b, i: (layer, 0, 0)),
            pl.BlockSpec((1, d, _W_GLA), lambda b, i: (layer, 0, _W_GLA_BLK)),
            pl.BlockSpec((1, d, LANE), lambda b, i: (layer, 0, 0)),
        ],
        out_specs=pl.BlockSpec((1, tm, GLA_COLS), lambda b, i: (b, i, 0)),
        compiler_params=pltpu.CompilerParams(
            dimension_semantics=("arbitrary", "arbitrary"), vmem_limit_bytes=48 * 1024 * 1024),
        name="inproj_gla",
    )(x, mod, g, w_all, w_lr)


def _inproj_qkuv_kernel(x_ref, mod_ref, g_ref, wu_ref, wk_ref, wqvt_ref, kg_ref, o_ref, qt_ref, vt_ref):
    b = pl.program_id(0)
    h = _norm_mod(x_ref[0], g_ref[0], _mod_rows(mod_ref, b, 1), _mod_rows(mod_ref, b, 0)).astype(BF16)
    tn = _W_TILE
    o_ref[0, :, 0:tn] = _seg_rmsnorm(_dot(h, wk_ref[0]), kg_ref[0], 1.0).astype(o_ref.dtype)
    o_ref[0, :, tn:] = _dot(h, wu_ref[0]).astype(o_ref.dtype)
    qvt = _dot_nt(wqvt_ref[0], h)
    qscale = DIFF_DH ** -0.5 * math.log2(math.e)
    for seg in range(tn // DIFF_DH):
        rows = slice(seg * DIFF_DH, (seg + 1) * DIFF_DH)
        qs = qvt[rows]
        ms = jnp.mean(qs * qs, axis=0, keepdims=True)
        qt_ref[0, 0, rows, :] = (qs * (lax.rsqrt(ms + EPS) * qscale)).astype(qt_ref.dtype)
    vt = qvt[tn:].astype(vt_ref.dtype)
    tk = vt_ref.shape[3]
    for n in range(vt_ref.shape[1]):
        vt_ref[0, n] = vt[:, n * tk:(n + 1) * tk]


def _inproj_qkuv(x, mod, layer, g, w_all, w_qvt, k_gain):
    bsz, s, d = x.shape
    tm = min(TQ, s)
    tk = min(TK, s)
    tn = _W_TILE
    resident = lambda shape, *idx: pl.BlockSpec((1,) + shape, lambda b, i: (layer,) + idx,
                                                pipeline_mode=pl.Buffered(1))
    return pl.pallas_call(
        _inproj_qkuv_kernel,
        out_shape=(jax.ShapeDtypeStruct((bsz, s, 2 * tn), BF16),
                   jax.ShapeDtypeStruct((bsz, s // tm, tn, tm), BF16),
                   jax.ShapeDtypeStruct((bsz, s // tk, tn, tk), BF16)),
        grid=(bsz, s // tm),
        in_specs=[
            pl.BlockSpec((1, tm, d), lambda b, i: (b, i, 0)),
            pl.BlockSpec((1, 8, mod.shape[2]), lambda b, i: (layer, 0, 0)),
            resident((1, d), 0, 0),
            resident((d, tn), 0, _W_U_BLK), resident((d, tn), 0, _W_U_BLK + 2),
            resident((2 * tn, d), 0, 0), resident((1, LANE), 0, 0),
        ],
        out_specs=(
            pl.BlockSpec((1, tm, 2 * tn), lambda b, i: (b, i, 0)),
            pl.BlockSpec((1, 1, tn, tm), lambda b, i: (b, i, 0, 0)),
            pl.BlockSpec((1, tm // tk, tn, tk), lambda b, i: (b, i, 0, 0)),
        ),
        compiler_params=pltpu.CompilerParams(
            dimension_semantics=("arbitrary", "arbitrary"), vmem_limit_bytes=48 * 1024 * 1024),
        name="inproj_qkuv",
    )(x, mod, g, w_all, w_all, w_qvt, k_gain)


def _gla_kernel(x_ref, mod_ref, g_ref, w_ref, wlr_ref, gkw_ref, gkb_ref, ng_ref, ltri_ref, lall_ref,
                o_ref, s_scr, *, nchunk):
    @pl.when(pl.program_id(1) == 0)
    def _():
        s_scr[...] = jnp.zeros_like(s_scr)

    b = pl.program_id(0)
    h = _norm_mod(x_ref[0], g_ref[0], _mod_rows(mod_ref, b, 1), _mod_rows(mod_ref, b, 0)).astype(BF16)
    blk = _dot(h, w_ref[0])
    lr = _dot(h, wlr_ref[0])
    q = blk[:, 0:GLA_QK]
    k = blk[:, GLA_QK:2 * GLA_QK]
    v = blk[:, 2 * GLA_QK:2 * GLA_QK + GLA_V]
    r = blk[:, 2 * GLA_QK + GLA_V:2 * GLA_QK + 2 * GLA_V]
    z = _dot(lr.astype(BF16), gkw_ref[...]) + gkb_ref[...]
    la = (jnp.minimum(z, 0.0) - jnp.log1p(jnp.exp(-jnp.abs(z)))) * (1.0 / GLA_TAU)
    la_hi = la.astype(BF16)
    la_lo = (la - la_hi.astype(F32)).astype(BF16)
    ltri = ltri_ref[...]
    lall = lall_ref[...]
    bcum = _dot(ltri, la_hi) + _dot(ltri, la_lo)
    btot = _dot(lall, la_hi) + _dot(lall, la_lo)
    qd = q * (GLA_DK ** -0.5) * jnp.exp(bcum)
    kd = (k * jnp.exp(-bcum)).astype(BF16)
    ke = (k * jnp.exp(btot - bcum)).astype(BF16)
    vb = v.astype(BF16)
    gate = r * jax.nn.sigmoid(r)

    lane_head = lax.broadcasted_iota(jnp.int32, (1, GLA_QK), 1) // GLA_DK
    rowi = lax.broadcasted_iota(jnp.int32, (GLA_QK, GLA_CHUNK), 0) % GLA_CHUNK
    colj = lax.broadcasted_iota(jnp.int32, (GLA_QK, GLA_CHUNK), 1)
    tril = rowi >= colj
    ones_cv = jnp.ones((GLA_CHUNK, GLA_DV), BF16)
    ng = ng_ref[...]

    chunks = [slice(c * GLA_CHUNK, (c + 1) * GLA_CHUNK) for c in range(nchunk)]

    states = []
    state = s_scr[...]
    for rows in chunks:
        states.append(state.astype(BF16))
        kv = _dot_tn(ke[rows], vb[rows])
        kvd = jnp.concatenate(
            [kv[h * GLA_DK:(h + 1) * GLA_DK, h * GLA_DV:(h + 1) * GLA_DV] for h in range(GLA_HEADS)],
            axis=0)
        ldec = _dot_tn(la_hi[rows], ones_cv) + _dot_tn(la_lo[rows], ones_cv)
        state = jnp.exp(ldec) * state + kvd
    s_scr[...] = state

    qms = [jnp.concatenate(
        [jnp.where(lane_head == h, qd[rows], 0.0).astype(BF16) for h in range(GLA_HEADS)], axis=0)
        for rows in chunks]
    atts = [_dot_nt(qm, kd[rows]) for qm, rows in zip(qms, chunks)]
    inters = [_dot(qm, st) for qm, st in zip(qms, states)]
    atts = [jnp.where(tril, a, 0.0).astype(BF16) for a in atts]
    heads = [slice(h * GLA_CHUNK, (h + 1) * GLA_CHUNK) for h in range(GLA_HEADS)]
    outs = [[_dot(att[hr], vb[rows][:, h * GLA_DV:(h + 1) * GLA_DV]) + inter[hr]
             for h, hr in enumerate(heads)]
            for att, inter, rows in zip(atts, inters, chunks)]
    for o_heads, rows in zip(outs, chunks):
        normed = [o_h * lax.rsqrt(jnp.mean(o_h * o_h, axis=-1, keepdims=True) + EPS) * ng for o_h in o_heads]
        o = jnp.concatenate(normed, axis=1) * gate[rows]
        o_ref[0, rows, :] = o.astype(o_ref.dtype)


def _gla(x, mod, layer, g, w_all, w_lr, gk_w, gk_b, norm_g):
    bsz, s, d = x.shape
    tg = min(TG, s)
    nchunk = tg // GLA_CHUNK
    gkw = jnp.zeros((LANE, GLA_QK), F32).at[:GLA_LR].set(gk_w).astype(BF16)
    ri = jnp.arange(tg)[:, None]
    ci = jnp.arange(tg)[None, :]
    same = (ri // GLA_CHUNK) == (ci // GLA_CHUNK)
    ltri = (same & (ri >= ci)).astype(BF16)
    lall = same.astype(BF16)
    const = lambda shape: pl.BlockSpec(shape, lambda b, t: (0,) * len(shape))
    return pl.pallas_call(
        functools.partial(_gla_kernel, nchunk=nchunk),
        out_shape=jax.ShapeDtypeStruct((bsz, s, GLA_V), BF16),
        grid=(bsz, s // tg),
        in_specs=[
            pl.BlockSpec((1, tg, d), lambda b, t: (b, t, 0)),
            pl.BlockSpec((1, 8, mod.shape[2]), lambda b, t: (layer, 0, 0)),
            pl.BlockSpec((1, 1, d), lambda b, t: (layer, 0, 0)),
            pl.BlockSpec((1, d, _W_GLA), lambda b, t: (layer, 0, _W_GLA_BLK)),
            pl.BlockSpec((1, d, LANE), lambda b, t: (layer, 0, 0)),
            const((LANE, GLA_QK)), const((1, GLA_QK)), const((1, GLA_DV)),
            const((tg, tg)), const((tg, tg)),
        ],
        out_specs=pl.BlockSpec((1, tg, GLA_V), lambda b, t: (b, t, 0)),
        scratch_shapes=[pltpu.VMEM((GLA_QK, GLA_DV), F32)],
        compiler_params=pltpu.CompilerParams(
            dimension_semantics=("arbitrary", "arbitrary"), vmem_limit_bytes=48 * 1024 * 1024),
        name="gla",
    )(x, mod, g, w_all, w_lr, gkw, gk_b.reshape(1, GLA_QK), norm_g.reshape(1, GLA_DV), ltri, lall)


_S5_COLS = S5_LANES // LANE
_S5_CG = 8
_S5_SEG = 8


def _s5_tables(lam_re, lam_im, b_re, b_im, c_re, c_im, log_dt, sl):
    nl = lam_re.shape[0]
    dt = jnp.exp(log_dt.astype(F32))[..., None]
    ar = (lam_re.astype(F32) * dt).reshape(nl, 1, S5_LANES)
    ai = (lam_im.astype(F32) * dt).reshape(nl, 1, S5_LANES)

    def powers(ks):
        kk = jnp.asarray(ks, F32)[None, :, None]
        mag = jnp.exp(kk * ar)
        return mag * jnp.cos(kk * ai), mag * jnp.sin(kk * ai)

    lr1, li1 = powers([1.0])
    lbr, lbi = lr1.reshape(lam_re.shape), li1.reshape(lam_re.shape)
    den = lam_re * lam_re + lam_im * lam_im
    nr = ((lbr - 1.0) * lam_re + lbi * lam_im) / den
    ni = (lbi * lam_re - (lbr - 1.0) * lam_im) / den
    bbr = nr[..., None] * b_re - ni[..., None] * b_im
    bbi = nr[..., None] * b_im + ni[..., None] * b_re
    eye4 = jnp.eye(4, dtype=F32)
    band = eye4[np.arange(8) % 4]

    def b_tiles(bb):
        bb4 = bb.reshape(nl, 8, 4, S5_STATE, S5_GROUP)
        small = jnp.einsum('lnkph,kj->lnkhjp', bb4, eye4).reshape(nl, 8, 64, 256)
        return jnp.einsum('lnrc,nq->lnqrc', small, band).reshape(nl, 8, 256, 256)

    def c_tiles(cc):
        cc4 = cc.reshape(nl, 8, 4, S5_GROUP, S5_STATE)
        small = jnp.einsum('lnkhp,kj->lnkpjh', cc4, eye4).reshape(nl, 8, 256, 64)
        return jnp.einsum('lnrc,nq->lnrqc', small, band).reshape(nl, 8, 256, 256)

    bt = jnp.concatenate([b_tiles(bbr), b_tiles(bbi)], axis=1).astype(BF16)
    ct = jnp.concatenate([c_tiles(c_re), c_tiles(-c_im)], axis=1).astype(BF16)
    lam8 = jnp.concatenate([jnp.broadcast_to(lr1, (nl, _S5_SEG, S5_LANES)),
                            jnp.broadcast_to(li1, (nl, _S5_SEG, S5_LANES))], axis=2)
    dsteps = np.array([1, 2, 4])
    keep = jnp.asarray(np.arange(_S5_SEG)[None, :] >= dsteps[:, None], F32)
    pr3, pi3 = powers(dsteps * float(sl))
    logstep = jnp.stack([keep[None, :, :, None] * pr3[:, :, None, :],
                         keep[None, :, :, None] * pi3[:, :, None, :]], axis=2)
    logstep = logstep.transpose(0, 3, 1, 2, 4).reshape(nl, _S5_SEG, 6 * S5_LANES)
    qr, qi = powers(np.arange(1, _S5_SEG + 1) * float(sl))
    seg_tab = jnp.concatenate([logstep, qr, qi], axis=2)
    pwr, pwi = powers(np.arange(1, sl + 1))
    pw = jnp.concatenate([pwr, pwi], axis=2)
    return bt, ct, lam8, seg_tab, pw


def _s5_kernel(u_ref, bt_ref, ct_ref, lam_ref, seg_ref, pw_ref, d_ref, gw_ref, gb_ref, o_ref,
               bu_scr, c_scr, x_scr, car_scr, *, ts):
    sl = ts // _S5_SEG
    pitch = sl + 8

    @pl.when(pl.program_id(1) == 0)
    def _():
        car_scr[...] = jnp.zeros_like(car_scr)

    u = u_ref[0]
    for t in range(16):
        half = (t % 8) // 4
        res = _dot(u[:, half * 256:(half + 1) * 256], bt_ref[0, t])
        for s in range(_S5_SEG):
            for w in range(2):
                bu_scr[2 * t + w, s * pitch:s * pitch + sl, :] = res[s * sl:(s + 1) * sl, w * LANE:(w + 1) * LANE]

    def lanes(ref, idx, j):
        return ref[0, :, idx * S5_LANES + j * LANE: idx * S5_LANES + (j + 1) * LANE]

    for j0 in range(0, _S5_COLS, _S5_CG):
        cols = list(range(j0, j0 + _S5_CG))

        def body(t, carry, cols=cols):
            new = [None] * (2 * len(cols))
            for n, j in enumerate(cols):
                xr, xi = carry[n], carry[len(cols) + n]
                lr_, li_ = lanes(lam_ref, 0, j), lanes(lam_ref, 1, j)
                vr = bu_scr[j, pl.ds(t, _S5_SEG, stride=pitch), :]
                vi = bu_scr[_S5_COLS + j, pl.ds(t, _S5_SEG, stride=pitch), :]
                nr = lr_ * xr - li_ * xi + vr
                ni = lr_ * xi + li_ * xr + vi
                bu_scr[j, pl.ds(t, _S5_SEG, stride=pitch), :] = nr
                bu_scr[_S5_COLS + j, pl.ds(t, _S5_SEG, stride=pitch), :] = ni
                new[n], new[len(cols) + n] = nr, ni
            return tuple(new)

        zero = jnp.zeros((_S5_SEG, LANE), F32)
        fin = lax.fori_loop(0, sl, body, (zero,) * (2 * len(cols)), unroll=4)

        row = lax.broadcasted_iota(jnp.int32, (_S5_SEG, LANE), 0)
        for n, j in enumerate(cols):
            gr, gi = fin[n], fin[len(cols) + n]
            for si, dstep in enumerate((1, 2, 4)):
                tr, ti = lanes(seg_ref, 2 * si, j), lanes(seg_ref, 2 * si + 1, j)
                sr = pltpu.roll(gr, dstep, 0)
                sm = pltpu.roll(gi, dstep, 0)
                gr, gi = gr + tr * sr - ti * sm, gi + tr * sm + ti * sr
            cr = car_scr[:, j * LANE:(j + 1) * LANE]
            ci = car_scr[:, S5_LANES + j * LANE:S5_LANES + (j + 1) * LANE]
            qr, qi = lanes(seg_ref, 6, j), lanes(seg_ref, 7, j)
            gr, gi = gr + qr * cr - qi * ci, gi + qr * ci + qi * cr
            c_scr[j] = jnp.where(row == 0, cr, pltpu.roll(gr, 1, 0))
            c_scr[_S5_COLS + j] = jnp.where(row == 0, ci, pltpu.roll(gi, 1, 0))
            car_scr[:, j * LANE:(j + 1) * LANE] = jnp.broadcast_to(gr[_S5_SEG - 1:, :], (_S5_SEG, LANE))
            car_scr[:, S5_LANES + j * LANE:S5_LANES + (j + 1) * LANE] = jnp.broadcast_to(
                gi[_S5_SEG - 1:, :], (_S5_SEG, LANE))

    for j in range(_S5_COLS):
        pr_ = pw_ref[0, :, j * LANE:(j + 1) * LANE]
        pi_ = pw_ref[0, :, S5_LANES + j * LANE:S5_LANES + (j + 1) * LANE]
        for s in range(_S5_SEG):
            cr = c_scr[j, s:s + 1, :]
            ci = c_scr[_S5_COLS + j, s:s + 1, :]
            xr = bu_scr[j, s * pitch:s * pitch + sl, :] + pr_ * cr - pi_ * ci
            xi = bu_scr[_S5_COLS + j, s * pitch:s * pitch + sl, :] + pr_ * ci + pi_ * cr
            x_scr[s * sl:(s + 1) * sl, j * LANE:(j + 1) * LANE] = xr.astype(BF16)
            x_scr[s * sl:(s + 1) * sl, S5_LANES + j * LANE:S5_LANES + (j + 1) * LANE] = xi.astype(BF16)

    ys = []
    for half in range(2):
        acc = None
        for t in [half * 4 + i for i in range(4)] + [8 + half * 4 + i for i in range(4)]:
            part = _dot(x_scr[:, t * 256:(t + 1) * 256], ct_ref[0, t])
            acc = part if acc is None else acc + part
        ys.append(acc)
    y = jnp.concatenate(ys, axis=1) + d_ref[0] * u.astype(F32)
    g = 0.5 * y * (1.0 + jnp.tanh(math.sqrt(2.0 / math.pi) * (y + 0.044715 * (y * y * y))))
    zz = _dot(g.astype(BF16), gw_ref[0]) + gb_ref[0]
    o_ref[0] = (g * jax.nn.sigmoid(zz)).astype(o_ref.dtype)


def _s5(qku, layer, tables, d_skip, glu_w, glu_b):
    bsz, s, _ = qku.shape
    ts = min(TS, s)
    sl = ts // _S5_SEG
    bt, ct, lam8, seg_tab, pw = tables
    const = lambda shape: pl.BlockSpec((1,) + shape, lambda b, t: (layer,) + (0,) * len(shape))
    return pl.pallas_call(
        functools.partial(_s5_kernel, ts=ts),
        out_shape=jax.ShapeDtypeStruct((bsz, s, S5_WIDTH), BF16),
        grid=(bsz, s // ts),
        in_specs=[
            pl.BlockSpec((1, ts, S5_WIDTH), lambda b, t: (b, t, 1)),
            const((16, 256, 256)), const((16, 256, 256)),
            const((_S5_SEG, 2 * S5_LANES)), const((_S5_SEG, 8 * S5_LANES)), const((sl, 2 * S5_LANES)),
            const((1, S5_WIDTH)), const((S5_WIDTH, S5_WIDTH)), const((1, S5_WIDTH)),
        ],
        out_specs=pl.BlockSpec((1, ts, S5_WIDTH), lambda b, t: (b, t, 0)),
        scratch_shapes=[
            pltpu.VMEM((2 * _S5_COLS, _S5_SEG * (sl + 8), LANE), F32),
            pltpu.VMEM((2 * _S5_COLS, _S5_SEG, LANE), F32),
            pltpu.VMEM((ts, 2 * S5_LANES), BF16),
            pltpu.VMEM((_S5_SEG, 2 * S5_LANES), F32)],
        compiler_params=pltpu.CompilerParams(
            dimension_semantics=("arbitrary", "arbitrary"), vmem_limit_bytes=48 * 1024 * 1024),
        name="s5",
    )(qku, bt, ct, lam8, seg_tab, pw, d_skip, glu_w, glu_b)


_ATT_STRIP = 256
_ATT_AHEAD = 3
_ATT_ONES = 16


def _alibi_tables(tq, tk):
    parts, rem = [], math.log2(math.e)
    for _ in range(3):
        p = float(np.asarray(rem, dtype=BF16))
        parts.append(p)
        rem -= p
    qa = ((np.arange(tq) // 64) * 64).astype(np.float32)
    qb = (np.arange(tq) % 64).astype(np.float32)
    ka = ((np.arange(tk) // 64) * 64).astype(np.float32)
    kb = (np.arange(tk) % 64).astype(np.float32)
    qx = np.zeros((DIFF_HEADS, tq, LANE), np.float32)
    kx = np.zeros((DIFF_HEADS, tk, LANE), np.float32)
    cval = np.zeros((DIFF_HEADS, 8, LANE), np.float32)
    for h in range(DIFF_HEADS):
        slope = 2.0 ** (-8.0 * (h + 1) / DIFF_HEADS)
        for n, p in enumerate(parts):
            qx[h, :, n] = qx[h, :, 3 + n] = slope * p
            kx[h, :, n] = ka
            kx[h, :, 3 + n] = kb
            qx[h, :, 6 + n] = -qa
            qx[h, :, 9 + n] = -qb
            kx[h, :, 6 + n] = kx[h, :, 9 + n] = slope * p
        cval[h] = slope * math.log2(math.e)
    qxt = np.ascontiguousarray(qx.transpose(0, 2, 1))
    return jnp.asarray(qxt, BF16), jnp.asarray(kx, BF16), jnp.asarray(cval, F32)


def _attn_kernel(lo_ref, q_ref, k_ref, vt_ref, qx_ref, kx_ref, c_ref, lam_ref, sg_ref, o_ref,
                 qt_scr, m_scr, acc_scr, *, tq, tk, lambda_init, layer):
    qi = pl.program_id(2)
    lo = lo_ref[layer, pl.program_id(1), qi]
    c11 = c_ref[0, 0:1, 0:1]
    qt = q_ref[0, 0]
    row = lax.broadcasted_iota(jnp.int32, (LANE, 1), 0)
    zero = jnp.zeros_like(qt)
    for half in range(2):
        qt_scr[half, 0:LANE, :] = jnp.where((row >= DIFF_DH) == (half == 1), qt, zero)
        qt_scr[half, LANE:, :] = qx_ref[0]
    kx = kx_ref[0]
    nstrip = tq // _ATT_STRIP
    kpq = tq // tk
    ones_rows = jnp.ones((_ATT_ONES, tk), BF16)

    m_scr[...] = jnp.full_like(m_scr, NEG_BIG)
    acc_scr[...] = jnp.zeros_like(acc_scr)

    def blocks(entries):
        loaded = []
        for kj, diag in entries:
            start = pl.multiple_of(kj * tk, tk)
            ke = jnp.concatenate([k_ref[0, pl.ds(start, tk), :], kx], axis=1)
            vt = jnp.concatenate([vt_ref[0, kj], ones_rows], axis=0)
            shift = c11 * (kj * tk - qi * tq).astype(F32)
            loaded.append((ke, vt, shift))
        units = []
        for e, (kj, diag) in enumerate(entries):
            for half in range(2):
                for st in range(nstrip):
                    nk = tk if diag is None else min(tk, (st + 1) * _ATT_STRIP - diag * tk)
                    if nk > 0:
                        units.append((e, half, st, nk))

        def scores(unit):
            e, half, st, nk = unit
            return _dot(loaded[e][0][:nk], qt_scr[half, :, st * _ATT_STRIP:(st + 1) * _ATT_STRIP])

        pending = [scores(u) for u in units[:_ATT_AHEAD]]
        for n, (e, half, st, nk) in enumerate(units):
            s_t = pending.pop(0)
            if n + _ATT_AHEAD < len(units):
                pending.append(scores(units[n + _ATT_AHEAD]))
            _, vt, shift = loaded[e]
            diag = entries[e][1]
            cols = slice(st * _ATT_STRIP, (st + 1) * _ATT_STRIP)
            if diag is not None and diag * tk + nk - 1 > st * _ATT_STRIP:
                jrow = lax.broadcasted_iota(jnp.int32, (nk, _ATT_STRIP), 0)
                icol = lax.broadcasted_iota(jnp.int32, (nk, _ATT_STRIP), 1)
                s_t = jnp.where(jrow + diag * tk <= icol + st * _ATT_STRIP, s_t, NEG_BIG)
            m_old = m_scr[half, :, cols] - shift
            m_new = jnp.maximum(m_old, jnp.max(s_t, axis=0, keepdims=True))
            alpha = jnp.exp2(m_old - m_new)
            p = jnp.exp2(s_t - m_new)
            acc_scr[half, :, cols] = alpha * acc_scr[half, :, cols] + _dot(vt[:, :nk], p.astype(BF16))
            m_scr[half, :, cols] = m_new + shift

    def body(i, carry):
        blocks([(i * kpq + d, None) for d in range(kpq)])
        return carry

    if kpq == 2:
        @pl.when(lo % 2 == 1)
        def _():
            blocks([(lo, None)])

        first_group = (lo + 1) // 2
    else:
        first_group = lo // kpq
    lax.fori_loop(first_group, qi, body, 0)
    blocks([(qi * kpq + d, d) for d in range(kpq)])

    lam = (jnp.exp(jnp.sum(lam_ref[0, 0:1, :] * lam_ref[0, 1:2, :], axis=-1, keepdims=True))
           - jnp.exp(jnp.sum(lam_ref[0, 2:3, :] * lam_ref[0, 3:4, :], axis=-1, keepdims=True)) + lambda_init)
    norm = [acc_scr[half, 0:DIFF_DV, :] / acc_scr[half, DIFF_DV:DIFF_DV + 1, :] for half in range(2)]
    o_t = norm[0] - lam * norm[1]
    o = o_t.T
    ms = jnp.mean(o * o, axis=-1, keepdims=True)
    o = o * lax.rsqrt(ms + EPS) * sg_ref[0] * (1.0 - lambda_init)
    o_ref[0] = o.astype(o_ref.dtype)


_F32_EXP2_ZERO = 152.0


def _first_key_groups(qk_gain, seq, tq, tk):
    gmax = jnp.max(jnp.abs(qk_gain), axis=1)
    qnorm = 8.0 * DIFF_DH ** -0.5 * math.log2(math.e)
    bound = 1.02 * qnorm * 8.0 * gmax
    thresh = (2.0 * bound + _F32_EXP2_ZERO)[:, None, None]
    slopes = 2.0 ** (-8.0 * np.arange(1, DIFF_HEADS + 1) / DIFF_HEADS)
    cvals = jnp.asarray(slopes * math.log2(math.e), F32)[None, :, None]
    q0 = jnp.asarray(np.arange(seq // tq) * tq, F32)[None, None, :]
    kj_min = jnp.ceil((q0 + 1.0 - thresh / cvals) / tk - 1.0)
    kj_min = jnp.clip(kj_min, 0.0, float(seq // tk)).astype(jnp.int32)
    return jnp.minimum(kj_min, (q0 / tk).astype(jnp.int32))


def _diff_attention(ku, qt, vt, layer, first_groups, lam_rows, subln_g, lambda_init):
    bsz, s, _ = ku.shape
    tk = vt.shape[3]
    tq = qt.shape[3]
    qx, kx, cval = _alibi_tables(tq, tk)
    return pl.pallas_call(
        functools.partial(_attn_kernel, tq=tq, tk=tk, lambda_init=lambda_init, layer=layer),
        out_shape=jax.ShapeDtypeStruct((bsz, s, DIFF_HEADS * DIFF_DV), BF16),
        grid=(bsz, DIFF_HEADS, s // tq),
        in_specs=[
            pl.BlockSpec(memory_space=pltpu.SMEM),
            pl.BlockSpec((1, 1, LANE, tq), lambda b, h, i: (b, i, h, 0)),
            pl.BlockSpec((1, s, LANE), lambda b, h, i: (b, 0, h)),
            pl.BlockSpec((1, s // tk, DIFF_DV, tk), lambda b, h, i: (b, 0, h, 0)),
            pl.BlockSpec((1, LANE, tq), lambda b, h, i: (h, 0, 0)),
            pl.BlockSpec((1, tk, LANE), lambda b, h, i: (h, 0, 0)),
            pl.BlockSpec((1, 8, LANE), lambda b, h, i: (h, 0, 0)),
            pl.BlockSpec((1, 8, LANE), lambda b, h, i: (layer, 0, 0)),
            pl.BlockSpec((1, 1, LANE), lambda b, h, i: (layer, 0, 0)),
        ],
        out_specs=pl.BlockSpec((1, tq, LANE), lambda b, h, i: (b, i, h)),
        scratch_shapes=[
            pltpu.VMEM((2, 2 * LANE, tq), BF16), pltpu.VMEM((2, 1, tq), F32),
            pltpu.VMEM((2, DIFF_DV + _ATT_ONES, tq), F32)],
        compiler_params=pltpu.CompilerParams(
            dimension_semantics=("arbitrary", "arbitrary", "arbitrary"),
            vmem_limit_bytes=48 * 1024 * 1024),
        name="diff_attn",
    )(first_groups, qt, ku, vt, qx, kx, cval, lam_rows, subln_g)


def _merge_kernel(x_ref, mod_ref, g_ref, wg_ref, oa_ref, ob_ref, oc_ref, wa_ref, wb_ref, wc_ref,
                  wo_ref, o_ref):
    b = pl.program_id(0)
    x = x_ref[0]
    h = _norm_mod(x, g_ref[0], _mod_rows(mod_ref, b, 1), _mod_rows(mod_ref, b, 0)).astype(BF16)
    merged = None
    for n, (ob, wb) in enumerate(((oa_ref, wa_ref), (ob_ref, wb_ref), (oc_ref, wc_ref))):
        gate = jax.nn.sigmoid(_dot(h, wg_ref[0, :, n * D_MODEL:(n + 1) * D_MODEL]))
        term = gate * _dot(ob[0], wb[0])
        merged = term if merged is None else merged + term
    y = _dot(merged.astype(BF16), wo_ref[0])
    o_ref[0] = x + _mod_rows(mod_ref, b, 2) * y


def _merge(x, mod, layer, g, w_all, o_a, o_b, o_c, w_a, w_b, w_c, w_out):
    bsz, s, d = x.shape
    tm = min(TM_MERGE, s)
    lay = lambda shape: pl.BlockSpec((1,) + shape, lambda b, i: (layer,) + (0,) * len(shape))
    tok = lambda n: pl.BlockSpec((1, tm, n), lambda b, i: (b, i, 0))
    return pl.pallas_call(
        _merge_kernel,
        out_shape=jax.ShapeDtypeStruct((bsz, s, d), F32),
        grid=(bsz, s // tm),
        in_specs=[
            tok(d),
            lay((8, mod.shape[2])),
            lay((1, d)), lay((d, _W_GATES)),
            tok(GLA_V), tok(S5_WIDTH), tok(DIFF_HEADS * DIFF_DV),
            lay((GLA_V, d)), lay((S5_WIDTH, d)), lay((DIFF_HEADS * DIFF_DV, d)), lay((d, d)),
        ],
        out_specs=tok(d),
        compiler_params=pltpu.CompilerParams(
            dimension_semantics=("arbitrary", "arbitrary"), vmem_limit_bytes=56 * 1024 * 1024),
        name="merge",
    )(x, mod, g, w_all, o_a, o_b, o_c, w_a, w_b, w_c, w_out)


_HALO = 16


def _ffn_kernel(x_ref, mod_ref, g_ref, wup_ref, cw_ref, cb_ref, wd_ref,
                o_ref, h_scr, halo_scr, up_scr, act_scr, *, tm, tf, f):
    b = pl.program_id(0)

    @pl.when(pl.program_id(1) == 0)
    def _():
        halo_scr[...] = jnp.zeros_like(halo_scr)

    x = x_ref[0]
    h = _norm_mod(x, g_ref[0], _mod_rows(mod_ref, b, 4), _mod_rows(mod_ref, b, 3)).astype(BF16)
    h_scr[0:_HALO, :] = halo_scr[...]
    h_scr[_HALO:, :] = h
    halo_scr[...] = h[tm - _HALO:, :]
    hh = h_scr[...]

    def conv(part, col):
        cw = cw_ref[0, :, col:col + tf]
        taps = [cw[n:n + 1, :] * up_scr[part, pl.ds(_HALO - 2 + n, tm), :] for n in range(3)]
        return taps[0] + taps[1] + taps[2] + cb_ref[0, :, col:col + tf]

    for c in range(f // tf):
        for part in range(2):
            col = part * f + c * tf
            up_scr[part] = _dot(hh, wup_ref[0, :, col:col + tf])
        a = conv(0, c * tf)
        gg = conv(1, f + c * tf)
        act_scr[:, c * tf:(c + 1) * tf] = (a * jax.nn.sigmoid(a) * gg).astype(BF16)

    o_ref[0] = x + _mod_rows(mod_ref, b, 5) * _dot(act_scr[...], wd_ref[0])


def _ffn(x, mod, layer, g, w_up, conv_w, conv_b, w_down):
    bsz, s, d = x.shape
    f = w_down.shape[1]
    tm = min(TM_FFN, s)
    tf = TF
    resident = lambda shape: pl.BlockSpec((1,) + shape, lambda b, i: (layer,) + (0,) * len(shape),
                                          pipeline_mode=pl.Buffered(1))
    return pl.pallas_call(
        functools.partial(_ffn_kernel, tm=tm, tf=tf, f=f),
        out_shape=jax.ShapeDtypeStruct((bsz, s, d), F32),
        grid=(bsz, s // tm),
        in_specs=[
            pl.BlockSpec((1, tm, d), lambda b, i: (b, i, 0)),
            pl.BlockSpec((1, 8, mod.shape[2]), lambda b, i: (layer, 0, 0)),
            resident((1, d)), resident((d, 2 * f)), resident((3, 2 * f)), resident((1, 2 * f)),
            resident((f, d)),
        ],
        out_specs=pl.BlockSpec((1, tm, d), lambda b, i: (b, i, 0)),
        scratch_shapes=[
            pltpu.VMEM((tm + _HALO, d), BF16), pltpu.VMEM((_HALO, d), BF16),
            pltpu.VMEM((2, tm + _HALO, tf), F32), pltpu.VMEM((tm, f), BF16)],
        compiler_params=pltpu.CompilerParams(
            dimension_semantics=("arbitrary", "arbitrary"), vmem_limit_bytes=56 * 1024 * 1024),
        name="ffn",
    )(x, mod, g, w_up, conv_w, conv_b, w_down)


def kernel(x, c, ada_w, ada_b, norm1_g, w_in, gla_gk_w, gla_gk_b, gla_norm_g, s5_lambda_re, s5_lambda_im, s5_b_re, s5_b_im, s5_c_re, s5_c_im, s5_d, s5_log_dt, s5_glu_w, s5_glu_b, diff_q_norm_g, diff_k_norm_g, diff_lambda_q1, diff_lambda_k1, diff_lambda_q2, diff_lambda_k2, diff_subln_g, w_branch_gla, w_branch_s5, w_branch_diff, w_out, norm2_g, ffn_w_up, ffn_conv_w, ffn_conv_b, ffn_w_down):
    depth, d = norm1_g.shape
    mod = _modulation(c, ada_w, ada_b)

    off_dv = _OFF_DQ + 2 * DIFF_HEADS * 2 * DIFF_DH
    w_bf = w_in.astype(BF16)
    w_all = jnp.concatenate(
        [w_bf[:, :, _OFF_GATES:], w_bf[:, :, _OFF_GLA:_OFF_GLR], w_bf[:, :, _OFF_SU:_OFF_GATES]],
        axis=2)
    w_lr = jnp.pad(w_in[:, :, _OFF_GLR:_OFF_SU], ((0, 0), (0, 0), (0, LANE - GLA_LR))).astype(BF16)
    w_qvt = jnp.swapaxes(jnp.concatenate(
        [w_in[:, :, _OFF_DQ:_OFF_DQ + _W_TILE], w_in[:, :, off_dv:_OFF_GATES]], axis=2), 1, 2).astype(BF16)
    w_a, w_b, w_c = (w.astype(BF16) for w in (w_branch_gla, w_branch_s5, w_branch_diff))
    w_o = w_out.astype(BF16)
    w_up, w_dn = ffn_w_up.astype(BF16), ffn_w_down.astype(BF16)
    glu_w = s5_glu_w.astype(BF16)
    g1, g2 = norm1_g.reshape(depth, 1, d), norm2_g.reshape(depth, 1, d)
    qk_gain = diff_q_norm_g * diff_k_norm_g
    kg = jnp.tile(qk_gain, (1, 2)).reshape(depth, 1, LANE)
    first_groups = _first_key_groups(qk_gain, x.shape[1], min(TQ, x.shape[1]), min(TK, x.shape[1]))
    sg = diff_subln_g.reshape(depth, 1, DIFF_DV)
    lam_rows = jnp.pad(
        jnp.stack([diff_lambda_q1, diff_lambda_k1, diff_lambda_q2, diff_lambda_k2], axis=1),
        ((0, 0), (0, 4), (0, LANE - DIFF_DH)))
    seq = x.shape[1]
    s5_tabs = _s5_tables(s5_lambda_re, s5_lambda_im, s5_b_re, s5_b_im, s5_c_re, s5_c_im, s5_log_dt,
                         min(TS, seq) // _S5_SEG)
    s5_d3 = s5_d.reshape(depth, 1, S5_WIDTH)
    glu_b3 = s5_glu_b.reshape(depth, 1, S5_WIDTH)
    conv_b3 = ffn_conv_b.reshape(depth, 1, -1)

    for l in range(depth):
        lambda_init = 0.8 - 0.6 * math.exp(-0.3 * l)
        ku, qt, vt = _inproj_qkuv(x, mod, l, g1, w_all, w_qvt, kg)
        o_a = _gla(x, mod, l, g1, w_all, w_lr, gla_gk_w[l], gla_gk_b[l], gla_norm_g[l])
        o_b = _s5(ku, l, s5_tabs, s5_d3, glu_w, glu_b3)
        o_c = _diff_attention(ku, qt, vt, l, first_groups, lam_rows, sg, lambda_init)
        x = _merge(x, mod, l, g1, w_all, o_a, o_b, o_c, w_a, w_b, w_c, w_o)
        x = _ffn(x, mod, l, g2, w_up, ffn_conv_w, conv_b3, w_dn)
    return x
```
